```python
import jax, jax.numpy as jnp
from jax import lax
import numpy as np

D_MODEL = 1024
BATCH = 2
SEQ = 16384
DEPTH = 2
DEC_BATCH = 4
DEC_SEQ = 4096
PAST_LEN = 128

GRID_W = 64
Q_BLOCK = 128
ROPE_THETA = 10000.0
NORM_EPS = 1e-6
MLA_HEADS = 4
MLA_Q_LORA = 384
MLA_KV_LORA = 256
MLA_NOPE = 128
MLA_ROPE = 64
MLA_V = 128
GQA_HEADS = 4
GQA_KV_HEADS = 2
GQA_HEAD_DIM = 128
MIX_WIDTH = MLA_HEADS * MLA_V + GQA_HEADS * GQA_HEAD_DIM
D_FF = 4 * D_MODEL
IN_SPLITS = (MLA_Q_LORA, MLA_KV_LORA, MLA_ROPE,
             GQA_HEADS * GQA_HEAD_DIM, GQA_KV_HEADS * GQA_HEAD_DIM, GQA_KV_HEADS * GQA_HEAD_DIM)
IN_WIDTH = sum(IN_SPLITS)
IN_OFFSETS = [int(o) for o in np.cumsum(IN_SPLITS)[:-1]]

kernel_name = "hymba_mla_gqa_axial_encoder"


def rmsnorm(x, g):
    x32 = x.astype(jnp.float32)
    y = x32 * lax.rsqrt(jnp.mean(x32 * x32, axis=-1, keepdims=True) + NORM_EPS)
    return (y * g.astype(jnp.float32)).astype(x.dtype)


def rope1d(x, pos):
    d = x.shape[-1]
    freqs = ROPE_THETA ** (-jnp.arange(0, d, 2, dtype=jnp.float32) / d)
    ang = pos.astype(jnp.float32)[:, None] * freqs[None, :]
    cos = jnp.cos(ang)[None, :, None, :]
    sin = jnp.sin(ang)[None, :, None, :]
    x32 = x.astype(jnp.float32)
    x1, x2 = x32[..., : d // 2], x32[..., d // 2:]
    return jnp.concatenate([x1 * cos - x2 * sin, x2 * cos + x1 * sin], axis=-1)


def axial_rope(x, rows):
    half = x.shape[-1] // 2
    row_pos = jnp.repeat(jnp.arange(rows), GRID_W)
    col_pos = jnp.tile(jnp.arange(GRID_W), rows)
    out = jnp.concatenate([rope1d(x[..., :half], row_pos), rope1d(x[..., half:], col_pos)], axis=-1)
    return out.astype(x.dtype)


def block_attention(q, k, v, scale):
    B, S, Hk, G, D = q.shape
    nb = S // Q_BLOCK
    qb = q.reshape(B, nb, Q_BLOCK, Hk, G, D).transpose(1, 0, 2, 3, 4, 5)

    def one_block(qblk):
        s = jnp.einsum('bqhgd,bkhd->bhgqk', qblk, k, preferred_element_type=jnp.float32) * scale
        p = jax.nn.softmax(s, axis=-1).astype(v.dtype)
        return jnp.einsum('bhgqk,bkhd->bqhgd', p, v)

    o = lax.map(one_block, qb)
    return o.transpose(1, 0, 2, 3, 4, 5).reshape(B, S, Hk * G, v.shape[-1])


def encoder_layer(x, attn_norm, w_in, mla_q_norm, w_mla_q_up, mla_kv_norm, w_mla_kv_up,
                  gqa_q_norm, gqa_k_norm, w_out, mlp_norm, w_mlp_up, w_mlp_down):
    B, S, _ = x.shape
    rows = S // GRID_W
    h = rmsnorm(x, attn_norm)
    z = h @ w_in
    cq, ckv, k_rope_raw, qb, kb, vb = jnp.split(z, IN_OFFSETS, axis=-1)

    qa = (rmsnorm(cq, mla_q_norm) @ w_mla_q_up).reshape(B, S, MLA_HEADS, MLA_NOPE + MLA_ROPE)
    qa_nope, qa_rope = qa[..., :MLA_NOPE], axial_rope(qa[..., MLA_NOPE:], rows)
    kva = (rmsnorm(ckv, mla_kv_norm) @ w_mla_kv_up).reshape(B, S, MLA_HEADS, MLA_NOPE + MLA_V)
    ka_nope, va = kva[..., :MLA_NOPE], kva[..., MLA_NOPE:]
    ka_rope = axial_rope(k_rope_raw.reshape(B, S, 1, MLA_ROPE), rows)
    q_a = jnp.concatenate([qa_nope, qa_rope], axis=-1)[:, :, :, None, :]
    k_a = jnp.concatenate([ka_nope, jnp.broadcast_to(ka_rope, (B, S, MLA_HEADS, MLA_ROPE))], axis=-1)
    o_a = block_attention(q_a, k_a, va, (MLA_NOPE + MLA_ROPE) ** -0.5)
    o_a = o_a.reshape(B, S, MLA_HEADS * MLA_V)

    q_b = axial_rope(rmsnorm(qb.reshape(B, S, GQA_HEADS, GQA_HEAD_DIM), gqa_q_norm), rows)
    k_b = axial_rope(rmsnorm(kb.reshape(B, S, GQA_KV_HEADS, GQA_HEAD_DIM), gqa_k_norm), rows)
    v_b = vb.reshape(B, S, GQA_KV_HEADS, GQA_HEAD_DIM)
    q_b = q_b.reshape(B, S, GQA_KV_HEADS, GQA_HEADS // GQA_KV_HEADS, GQA_HEAD_DIM)
    o_b = block_attention(q_b, k_b, v_b, GQA_HEAD_DIM ** -0.5).reshape(B, S, GQA_HEADS * GQA_HEAD_DIM)

    x = x + jnp.concatenate([o_a, o_b], axis=-1) @ w_out

    u = rmsnorm(x, mlp_norm) @ w_mlp_up
    x = x + jnp.square(jax.nn.relu(u)) @ w_mlp_down
    return x


def trunk(x, attn_norm, w_in, mla_q_norm, w_mla_q_up, mla_kv_norm, w_mla_kv_up,
          gqa_q_norm, gqa_k_norm, w_out, mlp_norm, w_mlp_up, w_mlp_down, final_norm):
    for l in range(DEPTH):
        x = encoder_layer(x, attn_norm[l], w_in[l], mla_q_norm[l], w_mla_q_up[l], mla_kv_norm[l],
                          w_mla_kv_up[l], gqa_q_norm[l], gqa_k_norm[l], w_out[l], mlp_norm[l],
                          w_mlp_up[l], w_mlp_down[l])
    return rmsnorm(x, final_norm)


def setup_inputs(seed: int = 0) -> dict:
    key = jax.random.key(seed)
    ks = jax.random.split(key, 16)

    def w(k, shape):
        return jax.random.normal(k, shape, jnp.float32) * (shape[-2] ** -0.5)

    def gain(k, shape):
        return 1.0 + 0.01 * jax.random.normal(k, shape, jnp.float32)

    return {
        "x_prompt": jax.random.normal(ks[0], (BATCH, SEQ, D_MODEL), jnp.float32),
        "x_sample": jax.random.normal(ks[1], (DEC_BATCH, DEC_SEQ, D_MODEL), jnp.float32),
        "attn_norm": gain(ks[2], (DEPTH, D_MODEL)),
        "w_in": w(ks[3], (DEPTH, D_MODEL, IN_WIDTH)),
        "mla_q_norm": gain(ks[4], (DEPTH, MLA_Q_LORA)),
        "w_mla_q_up": w(ks[5], (DEPTH, MLA_Q_LORA, MLA_HEADS * (MLA_NOPE + MLA_ROPE))),
        "mla_kv_norm": gain(ks[6], (DEPTH, MLA_KV_LORA)),
        "w_mla_kv_up": w(ks[7], (DEPTH, MLA_KV_LORA, MLA_HEADS * (MLA_NOPE + MLA_V))),
        "gqa_q_norm": gain(ks[8], (DEPTH, GQA_HEAD_DIM)),
        "gqa_k_norm": gain(ks[9], (DEPTH, GQA_HEAD_DIM)),
        "w_out": w(ks[10], (DEPTH, MIX_WIDTH, D_MODEL)),
        "mlp_norm": gain(ks[11], (DEPTH, D_MODEL)),
        "w_mlp_up": w(ks[12], (DEPTH, D_MODEL, D_FF)),
        "w_mlp_down": w(ks[13], (DEPTH, D_FF, D_MODEL)),
        "final_norm": gain(ks[14], (D_MODEL,)),
    }


def reference(x_prompt, x_sample, attn_norm, w_in, mla_q_norm, w_mla_q_up, mla_kv_norm, w_mla_kv_up,
              gqa_q_norm, gqa_k_norm, w_out, mlp_norm, w_mlp_up, w_mlp_down, final_norm):
    y_prompt = trunk(x_prompt, attn_norm, w_in, mla_q_norm, w_mla_q_up, mla_kv_norm, w_mla_kv_up,
                     gqa_q_norm, gqa_k_norm, w_out, mlp_norm, w_mlp_up, w_mlp_down, final_norm)
    y_sample = trunk(x_sample, attn_norm, w_in, mla_q_norm, w_mla_q_up, mla_kv_norm, w_mla_kv_up,
                     gqa_q_norm, gqa_k_norm, w_out, mlp_norm, w_mlp_up, w_mlp_down, final_norm)
    return (y_prompt, y_sample)
```

```python
import functools
import math

import jax
import jax.numpy as jnp
from jax import lax
from jax.experimental import pallas as pl
from jax.experimental.pallas import tpu as pltpu

D_MODEL = 1024
GRID_W = 64
ROPE_THETA = 10000.0
NORM_EPS = 1e-6
MLA_HEADS = 4
MLA_Q_LORA = 384
MLA_KV_LORA = 256
MLA_NOPE = 128
MLA_ROPE = 64
MLA_V = 128
GQA_HEADS = 4
GQA_KV_HEADS = 2
GQA_HEAD_DIM = 128
D_FF = 4 * D_MODEL
DEPTH = 2

LANES = 128
MXU_DIM = 256
VMEM_LIMIT_BYTES = 56 * 1024 * 1024

PROJ_TM = 512
MLP_TM = 512
KV_CHUNK = 512
Q_SUB = MXU_DIM
MLA_TQ = 512
GQA_TQ = 256
FF_CHUNK = 1024

MLA_DK = 2 * LANES
IN_WIDTH_PADDED = 1792
LOG2E = math.log2(math.e)
NEG_BIG = -1e30

_BF16 = jnp.bfloat16
_F32 = jnp.float32


def _rms(x, g):
    ms = jnp.mean(x * x, axis=-1, keepdims=True)
    return x * lax.rsqrt(ms + NORM_EPS) * g


def _rotary(x, cos, sin_lo, sin_hi, half):
    return x * cos + pltpu.roll(x, LANES - half, 1) * sin_lo + pltpu.roll(x, half, 1) * sin_hi


def _proj_kernel(x_ref, g_attn_ref, w_in_ref, g_q_ref, w_qup_ref, g_kv_ref, w_kvup_ref,
                 g_gq_ref, g_gk_ref, tab_ref,
                 qaT_ref, ka_ref, vaT_ref, qbT_ref, kb_ref, vbT_ref, *, n_chunks):
    tc = KV_CHUNK
    h = _rms(x_ref[...], g_attn_ref[...])
    z = jnp.dot(h.astype(_BF16), w_in_ref[...], preferred_element_type=_F32)
    o_ckv = MLA_Q_LORA
    o_qb = o_ckv + MLA_KV_LORA
    o_kb = o_qb + GQA_HEADS * GQA_HEAD_DIM
    o_vb = o_kb + GQA_KV_HEADS * GQA_HEAD_DIM
    o_kr = o_vb + GQA_KV_HEADS * GQA_HEAD_DIM
    cq = z[:, :o_ckv]
    ckv = z[:, o_ckv:o_qb]

    tab = tab_ref[...]
    g_cos, g_slo, g_shi = tab[:, 0:128], tab[:, 128:256], tab[:, 256:384]
    m_cos, m_slo, m_shi = tab[:, 384:512], tab[:, 512:640], tab[:, 640:768]
    rope_g = functools.partial(_rotary, cos=g_cos, sin_lo=g_slo, sin_hi=g_shi, half=GQA_HEAD_DIM // 4)
    rope_m = functools.partial(_rotary, cos=m_cos, sin_lo=m_slo, sin_hi=m_shi, half=MLA_ROPE // 4)

    scale_a = (MLA_NOPE + MLA_ROPE) ** -0.5 * LOG2E
    scale_b = GQA_HEAD_DIM ** -0.5 * LOG2E

    def store_t(ref, head, val):
        for c in range(n_chunks):
            ref[0, head, c] = val[c * tc:(c + 1) * tc, :].T.astype(_BF16)

    qa = jnp.dot(_rms(cq, g_q_ref[...]).astype(_BF16), w_qup_ref[...],
                 preferred_element_type=_F32)
    kva = jnp.dot(_rms(ckv, g_kv_ref[...]).astype(_BF16), w_kvup_ref[...],
                  preferred_element_type=_F32)
    k_rope = rope_m(z[:, o_kr:o_kr + LANES]).astype(_BF16)
    for hd in range(MLA_HEADS):
        base = hd * MLA_DK
        q_nope = qa[:, base:base + LANES] * scale_a
        q_rope = rope_m(qa[:, base + LANES:base + 2 * LANES]) * scale_a
        qaT_ref[0, hd, 0:LANES, :] = q_nope.T.astype(_BF16)
        qaT_ref[0, hd, LANES:2 * LANES, :] = q_rope.T.astype(_BF16)
        ka_ref[0, hd, :, 0:LANES] = kva[:, base:base + LANES].astype(_BF16)
        ka_ref[0, hd, :, LANES:2 * LANES] = k_rope
        store_t(vaT_ref, hd, kva[:, base + LANES:base + 2 * LANES])

    g_gq = g_gq_ref[...]
    g_gk = g_gk_ref[...]
    for hd in range(GQA_HEADS):
        qh = z[:, o_qb + hd * LANES:o_qb + (hd + 1) * LANES]
        qbT_ref[0, hd] = (rope_g(_rms(qh, g_gq)) * scale_b).T.astype(_BF16)
    for hd in range(GQA_KV_HEADS):
        kh = z[:, o_kb + hd * LANES:o_kb + (hd + 1) * LANES]
        kb_ref[0, hd] = rope_g(_rms(kh, g_gk)).astype(_BF16)
        store_t(vbT_ref, hd, z[:, o_vb + hd * LANES:o_vb + (hd + 1) * LANES])


def _const_spec(shape):
    nd = len(shape)
    return pl.BlockSpec(shape, lambda *_: (0,) * nd)


def _project(x, g_attn, w_in, g_q, w_qup, g_kv, w_kvup, g_gq, g_gk, tab):
    B, S, D = x.shape
    tm = PROJ_TM
    n_chunks = tm // KV_CHUNK
    ns = S // tm
    nc = S // KV_CHUNK
    out_shape = (
        jax.ShapeDtypeStruct((B, MLA_HEADS, MLA_DK, S), _BF16),
        jax.ShapeDtypeStruct((B, MLA_HEADS, S, MLA_DK), _BF16),
        jax.ShapeDtypeStruct((B, MLA_HEADS, nc, MLA_V, KV_CHUNK), _BF16),
        jax.ShapeDtypeStruct((B, GQA_HEADS, GQA_HEAD_DIM, S), _BF16),
        jax.ShapeDtypeStruct((B, GQA_KV_HEADS, S, GQA_HEAD_DIM), _BF16),
        jax.ShapeDtypeStruct((B, GQA_KV_HEADS, nc, GQA_HEAD_DIM, KV_CHUNK), _BF16),
    )
    in_specs = [
        pl.BlockSpec((None, tm, D), lambda b, s: (b, s, 0)),
        _const_spec(g_attn.shape), _const_spec(w_in.shape), _const_spec(g_q.shape),
        _const_spec(w_qup.shape), _const_spec(g_kv.shape), _const_spec(w_kvup.shape),
        _const_spec(g_gq.shape), _const_spec(g_gk.shape),
        pl.BlockSpec((tm, tab.shape[1]), lambda b, s: (s, 0)),
    ]
    out_specs = (
        pl.BlockSpec((1, MLA_HEADS, MLA_DK, tm), lambda b, s: (b, 0, 0, s)),
        pl.BlockSpec((1, MLA_HEADS, tm, MLA_DK), lambda b, s: (b, 0, s, 0)),
        pl.BlockSpec((1, MLA_HEADS, n_chunks, MLA_V, KV_CHUNK), lambda b, s: (b, 0, s, 0, 0)),
        pl.BlockSpec((1, GQA_HEADS, GQA_HEAD_DIM, tm), lambda b, s: (b, 0, 0, s)),
        pl.BlockSpec((1, GQA_KV_HEADS, tm, GQA_HEAD_DIM), lambda b, s: (b, 0, s, 0)),
        pl.BlockSpec((1, GQA_KV_HEADS, n_chunks, GQA_HEAD_DIM, KV_CHUNK), lambda b, s: (b, 0, s, 0, 0)),
    )
    return pl.pallas_call(
        functools.partial(_proj_kernel, n_chunks=n_chunks),
        grid=(B, ns),
        in_specs=in_specs,
        out_specs=out_specs,
        out_shape=out_shape,
        compiler_params=pltpu.CompilerParams(
            dimension_semantics=("parallel", "parallel"), vmem_limit_bytes=VMEM_LIMIT_BYTES),
        name="proj",
    )(x, g_attn, w_in, g_q, w_qup, g_kv, w_kvup, g_gq, g_gk, tab)


def _attn_kernel(qT_ref, k_ref, vT_ref, o_ref, *, group, n_sub, n_kv):
    tc = KV_CHUNK
    dv = vT_ref.shape[3]
    chains = [(g, s) for g in range(group) for s in range(n_sub)]
    q_tiles = [qT_ref[0, g, :, s * Q_SUB:(s + 1) * Q_SUB] for g, s in chains]

    def body(j, carry):
        off = pl.multiple_of(j * tc, tc)
        k_blk = k_ref[0, 0, pl.ds(off, tc), :]
        v_blk = vT_ref[0, 0, j]
        new = []
        for q_t, (m, l, acc) in zip(q_tiles, carry):
            s_t = jnp.dot(k_blk, q_t, preferred_element_type=_F32)
            m_new = jnp.maximum(m, jnp.max(s_t, axis=0, keepdims=True))
            p = jnp.exp2(s_t - m_new)
            alpha = jnp.exp2(m - m_new)
            l_new = alpha * l + jnp.sum(p, axis=0, keepdims=True)
            pv = jnp.dot(v_blk, p.astype(_BF16), preferred_element_type=_F32)
            new.append((m_new, l_new, alpha * acc + pv))
        return tuple(new)

    init = tuple((jnp.full((1, Q_SUB), NEG_BIG, _F32), jnp.zeros((1, Q_SUB), _F32),
                  jnp.zeros((dv, Q_SUB), _F32)) for _ in chains)
    final = lax.fori_loop(0, n_kv, body, init)
    for (g, s), (m, l, acc) in zip(chains, final):
        o = acc / l
        o_ref[0, s * Q_SUB:(s + 1) * Q_SUB, g * dv:(g + 1) * dv] = o.T.astype(o_ref.dtype)


def _attention(qT, k, vT, *, tq):
    B, Hq, dk, S = qT.shape
    _, Hkv, n_kv, dv, tc = vT.shape
    group = Hq // Hkv
    n_sub = tq // Q_SUB
    return pl.pallas_call(
        functools.partial(_attn_kernel, group=group, n_sub=n_sub, n_kv=n_kv),
        grid=(B, Hkv, S // tq),
        in_specs=[
            pl.BlockSpec((1, group, dk, tq), lambda b, h, q: (b, h, 0, q)),
            pl.BlockSpec((1, 1, S, dk), lambda b, h, q: (b, h, 0, 0)),
            pl.BlockSpec((1, 1, n_kv, dv, tc), lambda b, h, q: (b, h, 0, 0, 0)),
        ],
        out_specs=pl.BlockSpec((1, tq, group * dv), lambda b, h, q: (b, q, h)),
        out_shape=jax.ShapeDtypeStruct((B, S, Hq * dv), _BF16),
        compiler_params=pltpu.CompilerParams(
            dimension_semantics=("parallel", "parallel", "parallel"),
            vmem_limit_bytes=VMEM_LIMIT_BYTES),
        name="attn",
    )(qT, k, vT)


def _mlp_kernel(x_ref, oa_ref, ob_ref, wo_a_ref, wo_b_ref, g_mlp_ref, w_up_ref, w_dn_ref, g_fin_ref,
                y_ref, *, final):
    x1 = (x_ref[...]
          + jnp.dot(oa_ref[...], wo_a_ref[...], preferred_element_type=_F32)
          + jnp.dot(ob_ref[...], wo_b_ref[...], preferred_element_type=_F32))
    hn = _rms(x1, g_mlp_ref[...]).astype(_BF16)
    y_ref[...] = x1
    for c in range(D_FF // FF_CHUNK):
        u = jnp.dot(hn, w_up_ref[:, c * FF_CHUNK:(c + 1) * FF_CHUNK], preferred_element_type=_F32)
        a = jnp.square(jnp.maximum(u, 0.0)).astype(_BF16)
        y_ref[...] += jnp.dot(a, w_dn_ref[c * FF_CHUNK:(c + 1) * FF_CHUNK, :], preferred_element_type=_F32)
    if final:
        y_ref[...] = _rms(y_ref[...], g_fin_ref[...])


def _single_buffered(shape):
    nd = len(shape)
    return pl.BlockSpec(shape, lambda *_: (0,) * nd, pipeline_mode=pl.Buffered(1))


def _mlp(x, oa, ob, wo_a, wo_b, g_mlp, w_up, w_dn, g_fin, *, final):
    B, S, D = x.shape
    tm = MLP_TM
    tok = lambda w: pl.BlockSpec((None, tm, w), lambda b, s: (b, s, 0))
    return pl.pallas_call(
        functools.partial(_mlp_kernel, final=final),
        grid=(B, S // tm),
        in_specs=[
            tok(D), tok(oa.shape[-1]), tok(ob.shape[-1]),
            _single_buffered(wo_a.shape), _single_buffered(wo_b.shape), _const_spec(g_mlp.shape),
            _single_buffered(w_up.shape), _single_buffered(w_dn.shape), _const_spec(g_fin.shape),
        ],
        out_specs=tok(D),
        out_shape=jax.ShapeDtypeStruct((B, S, D), _F32),
        compiler_params=pltpu.CompilerParams(
            dimension_semantics=("parallel", "parallel"), vmem_limit_bytes=VMEM_LIMIT_BYTES),
        name="mlp",
    )(x, oa, ob, wo_a, wo_b, g_mlp, w_up, w_dn, g_fin)


def _rotary_tables(S):
    t = jnp.arange(S)
    row = (t // GRID_W).astype(_F32)[:, None]
    col = (t % GRID_W).astype(_F32)[:, None]
    lane = jnp.arange(LANES)

    def tables(group):
        half = group // 2
        quarter = half // 2
        j = lane % quarter
        freqs = ROPE_THETA ** (-(2.0 * j.astype(_F32)) / half)
        pos = jnp.where((lane % group) < half, row, col)
        ang = pos * freqs[None, :]
        first = (lane % half) < quarter
        cos = jnp.cos(ang)
        sin = jnp.sin(ang)
        return cos, jnp.where(first, -sin, 0.0), jnp.where(first, 0.0, sin)

    return jnp.concatenate(tables(GQA_HEAD_DIM) + tables(MLA_ROPE), axis=1)


def _relayout_weights(w_in, w_mla_q_up):
    o = [0]
    for w in (MLA_Q_LORA, MLA_KV_LORA, MLA_ROPE, GQA_HEADS * GQA_HEAD_DIM,
              GQA_KV_HEADS * GQA_HEAD_DIM, GQA_KV_HEADS * GQA_HEAD_DIM):
        o.append(o[-1] + w)
    cq, ckv, kr, qb, kb, vb = (w_in[..., o[i]:o[i + 1]] for i in range(6))
    pad = jnp.zeros(w_in.shape[:-1] + (LANES - MLA_ROPE,), w_in.dtype)
    w_in_p = jnp.concatenate([cq, ckv, qb, kb, vb, kr, pad], axis=-1).astype(_BF16)
    L, R, _ = w_mla_q_up.shape
    wq = w_mla_q_up.reshape(L, R, MLA_HEADS, MLA_NOPE + MLA_ROPE)
    wq = jnp.pad(wq, ((0, 0), (0, 0), (0, 0), (0, MLA_DK - MLA_NOPE - MLA_ROPE)))
    return w_in_p, wq.reshape(L, R, MLA_HEADS * MLA_DK).astype(_BF16)


def _trunk(x, p, tab):
    for l in range(DEPTH):
        qaT, ka, vaT, qbT, kb, vbT = _project(
            x, p["attn_norm"][l], p["w_in"][l], p["mla_q_norm"][l], p["w_qup"][l],
            p["mla_kv_norm"][l], p["w_kvup"][l], p["gqa_q_norm"][l], p["gqa_k_norm"][l], tab)
        oa = _attention(qaT, ka, vaT, tq=MLA_TQ)
        ob = _attention(qbT, kb, vbT, tq=GQA_TQ)
        x = _mlp(x, oa, ob, p["wo_a"][l], p["wo_b"][l], p["mlp_norm"][l], p["w_up"][l], p["w_dn"][l],
                 p["final_norm"], final=(l == DEPTH - 1))
    return x


def kernel(x_prompt, x_sample, attn_norm, w_in, mla_q_norm, w_mla_q_up, mla_kv_norm, w_mla_kv_up,
           gqa_q_norm, gqa_k_norm, w_out, mlp_norm, w_mlp_up, w_mlp_down, final_norm):
    w_in_p, w_qup = _relayout_weights(w_in, w_mla_q_up)
    row = lambda g: g[:, None, :]
    split = MLA_HEADS * MLA_V
    p = {
        "attn_norm": row(attn_norm), "w_in": w_in_p, "mla_q_norm": row(mla_q_norm), "w_qup": w_qup,
        "mla_kv_norm": row(mla_kv_norm), "w_kvup": w_mla_kv_up.astype(_BF16),
        "gqa_q_norm": row(gqa_q_norm), "gqa_k_norm": row(gqa_k_norm),
        "wo_a": w_out[:, :split, :].astype(_BF16), "wo_b": w_out[:, split:, :].astype(_BF16),
        "mlp_norm": row(mlp_norm), "w_up": w_mlp_up.astype(_BF16), "w_dn": w_mlp_down.astype(_BF16),
        "final_norm": final_norm[None, :],
    }
    tab = _rotary_tables(max(x_prompt.shape[1], x_sample.shape[1]))
    return _trunk(x_prompt, p, tab), _trunk(x_sample, p, tab)
```

```python
import functools
import math

import jax
import jax.numpy as jnp
from jax import lax
from jax.experimental import pallas as pl
from jax.experimental.pallas import tpu as pltpu

D_MODEL = 1024
GRID_W = 64
ROPE_THETA = 10000.0
NORM_EPS = 1e-6
MLA_HEADS = 4
MLA_Q_LORA = 384
MLA_KV_LORA = 256
MLA_NOPE = 128
MLA_ROPE = 64
MLA_V = 128
GQA_HEADS = 4
GQA_KV_HEADS = 2
GQA_HEAD_DIM = 128
D_FF = 4 * D_MODEL
DEPTH = 2

LANES = 128
MXU_DIM = 256
VMEM_LIMIT_BYTES = 56 * 1024 * 1024

PROJ_TM = 512
MLP_TM = 512
KV_CHUNK = 512
MLA_TQ = 512
GQA_TQ = 256
FF_CHUNK = 1024

MLA_DK = 2 * LANES
IN_WIDTH_PADDED = 1792
LOG2E = math.log2(math.e)
NEG_BIG = -1e30

_BF16 = jnp.bfloat16
_F32 = jnp.float32


def _rms(x, g):
    ms = jnp.mean(x * x, axis=-1, keepdims=True)
    return x * lax.rsqrt(ms + NORM_EPS) * g


def _rotary(x, cos, sin_lo, sin_hi, half):
    return x * cos + pltpu.roll(x, LANES - half, 1) * sin_lo + pltpu.roll(x, half, 1) * sin_hi


def _proj_kernel(x_ref, g_attn_ref, w_in_ref, g_q_ref, w_qup_ref, g_kv_ref, w_kvup_ref,
                 g_gq_ref, g_gk_ref, tab_ref,
                 qaT_ref, ka_ref, vaT_ref, qbT_ref, kb_ref, vbT_ref, *, n_chunks):
    tc = KV_CHUNK
    h = _rms(x_ref[...], g_attn_ref[...])
    z = jnp.dot(h.astype(_BF16), w_in_ref[...], preferred_element_type=_F32)
    o_ckv = MLA_Q_LORA
    o_qb = o_ckv + MLA_KV_LORA
    o_kb = o_qb + GQA_HEADS * GQA_HEAD_DIM
    o_vb = o_kb + GQA_KV_HEADS * GQA_HEAD_DIM
    o_kr = o_vb + GQA_KV_HEADS * GQA_HEAD_DIM
    cq = z[:, :o_ckv]
    ckv = z[:, o_ckv:o_qb]

    tab = tab_ref[...]
    g_cos, g_slo, g_shi = tab[:, 0:128], tab[:, 128:256], tab[:, 256:384]
    m_cos, m_slo, m_shi = tab[:, 384:512], tab[:, 512:640], tab[:, 640:768]
    rope_g = functools.partial(_rotary, cos=g_cos, sin_lo=g_slo, sin_hi=g_shi, half=GQA_HEAD_DIM // 4)
    rope_m = functools.partial(_rotary, cos=m_cos, sin_lo=m_slo, sin_hi=m_shi, half=MLA_ROPE // 4)

    scale_a = (MLA_NOPE + MLA_ROPE) ** -0.5 * LOG2E
    scale_b = GQA_HEAD_DIM ** -0.5 * LOG2E

    def store_t(ref, head, val):
        for c in range(n_chunks):
            ref[0, head, c] = val[c * tc:(c + 1) * tc, :].T.astype(_BF16)

    qa = jnp.dot(_rms(cq, g_q_ref[...]).astype(_BF16), w_qup_ref[...],
                 preferred_element_type=_F32)
    kva = jnp.dot(_rms(ckv, g_kv_ref[...]).astype(_BF16), w_kvup_ref[...],
                  preferred_element_type=_F32)
    k_rope = rope_m(z[:, o_kr:o_kr + LANES]).astype(_BF16)
    for hd in range(MLA_HEADS):
        base = hd * MLA_DK
        q_nope = qa[:, base:base + LANES] * scale_a
        q_rope = rope_m(qa[:, base + LANES:base + 2 * LANES]) * scale_a
        qaT_ref[0, hd, 0:LANES, :] = q_nope.T.astype(_BF16)
        qaT_ref[0, hd, LANES:2 * LANES, :] = q_rope.T.astype(_BF16)
        ka_ref[0, hd, :, 0:LANES] = kva[:, base:base + LANES].astype(_BF16)
        ka_ref[0, hd, :, LANES:2 * LANES] = k_rope
        store_t(vaT_ref, hd, kva[:, base + LANES:base + 2 * LANES])

    g_gq = g_gq_ref[...]
    g_gk = g_gk_ref[...]
    for hd in range(GQA_HEADS):
        qh = z[:, o_qb + hd * LANES:o_qb + (hd + 1) * LANES]
        qbT_ref[0, hd] = (rope_g(_rms(qh, g_gq)) * scale_b).T.astype(_BF16)
    for hd in range(GQA_KV_HEADS):
        kh = z[:, o_kb + hd * LANES:o_kb + (hd + 1) * LANES]
        kb_ref[0, hd] = rope_g(_rms(kh, g_gk)).astype(_BF16)
        store_t(vbT_ref, hd, z[:, o_vb + hd * LANES:o_vb + (hd + 1) * LANES])


def _const_spec(shape):
    nd = len(shape)
    return pl.BlockSpec(shape, lambda *_: (0,) * nd)


def _project(x, g_attn, w_in, g_q, w_qup, g_kv, w_kvup, g_gq, g_gk, tab):
    B, S, D = x.shape
    tm = PROJ_TM
    n_chunks = tm // KV_CHUNK
    ns = S // tm
    nc = S // KV_CHUNK
    out_shape = (
        jax.ShapeDtypeStruct((B, MLA_HEADS, MLA_DK, S), _BF16),
        jax.ShapeDtypeStruct((B, MLA_HEADS, S, MLA_DK), _BF16),
        jax.ShapeDtypeStruct((B, MLA_HEADS, nc, MLA_V, KV_CHUNK), _BF16),
        jax.ShapeDtypeStruct((B, GQA_HEADS, GQA_HEAD_DIM, S), _BF16),
        jax.ShapeDtypeStruct((B, GQA_KV_HEADS, S, GQA_HEAD_DIM), _BF16),
        jax.ShapeDtypeStruct((B, GQA_KV_HEADS, nc, GQA_HEAD_DIM, KV_CHUNK), _BF16),
    )
    in_specs = [
        pl.BlockSpec((None, tm, D), lambda b, s: (b, s, 0)),
        _const_spec(g_attn.shape), _const_spec(w_in.shape), _const_spec(g_q.shape),
        _const_spec(w_qup.shape), _const_spec(g_kv.shape), _const_spec(w_kvup.shape),
        _const_spec(g_gq.shape), _const_spec(g_gk.shape),
        pl.BlockSpec((tm, tab.shape[1]), lambda b, s: (s, 0)),
    ]
    out_specs = (
        pl.BlockSpec((1, MLA_HEADS, MLA_DK, tm), lambda b, s: (b, 0, 0, s)),
        pl.BlockSpec((1, MLA_HEADS, tm, MLA_DK), lambda b, s: (b, 0, s, 0)),
        pl.BlockSpec((1, MLA_HEADS, n_chunks, MLA_V, KV_CHUNK), lambda b, s: (b, 0, s, 0, 0)),
        pl.BlockSpec((1, GQA_HEADS, GQA_HEAD_DIM, tm), lambda b, s: (b, 0, 0, s)),
        pl.BlockSpec((1, GQA_KV_HEADS, tm, GQA_HEAD_DIM), lambda b, s: (b, 0, s, 0)),
        pl.BlockSpec((1, GQA_KV_HEADS, n_chunks, GQA_HEAD_DIM, KV_CHUNK), lambda b, s: (b, 0, s, 0, 0)),
    )
    return pl.pallas_call(
        functools.partial(_proj_kernel, n_chunks=n_chunks),
        grid=(B, ns),
        in_specs=in_specs,
        out_specs=out_specs,
        out_shape=out_shape,
        compiler_params=pltpu.CompilerParams(
            dimension_semantics=("parallel", "parallel"), vmem_limit_bytes=VMEM_LIMIT_BYTES),
        name="proj",
    )(x, g_attn, w_in, g_q, w_qup, g_kv, w_kvup, g_gq, g_gk, tab)


def _attn_kernel(qT_ref, k_ref, vT_ref, o_ref, s_a, s_b, p_a, p_b, acc_ref, *, group, n_kv):
    tc = KV_CHUNK
    tq = qT_ref.shape[3]
    dv = vT_ref.shape[3]
    q_t = jnp.concatenate([qT_ref[0, g] for g in range(group)], axis=1) if group > 1 else qT_ref[0, 0]

    def scores(j, s_ref):
        off = pl.multiple_of(j * tc, tc)
        s_ref[...] = jnp.dot(k_ref[0, 0, pl.ds(off, tc), :], q_t, preferred_element_type=_F32)

    def softmax(s_ref, p_ref, m, l):
        s_t = s_ref[...]
        m_new = jnp.maximum(m, jnp.max(s_t, axis=0, keepdims=True))
        p = jnp.exp2(s_t - m_new)
        alpha = jnp.exp2(m - m_new)
        p_ref[...] = p.astype(_BF16)
        return m_new, alpha * l + jnp.sum(p, axis=0, keepdims=True), alpha

    def weighted(j, p_ref, alpha):
        acc_ref[...] = alpha * acc_ref[...] + jnp.dot(vT_ref[0, 0, j], p_ref[...],
                                                      preferred_element_type=_F32)

    width = group * tq
    m = jnp.full((1, width), NEG_BIG, _F32)
    l = jnp.zeros((1, width), _F32)
    acc_ref[...] = jnp.zeros_like(acc_ref)
    scores(0, s_a)
    scores(1, s_b)
    m, l, alpha = softmax(s_a, p_a, m, l)

    def body(i, carry):
        m, l, alpha = carry
        j = 2 * i + 1
        scores(j + 1, s_a)
        m, l, alpha_n = softmax(s_b, p_b, m, l)
        weighted(j - 1, p_a, alpha)
        scores(j + 2, s_b)
        m, l, alpha = softmax(s_a, p_a, m, l)
        weighted(j, p_b, alpha_n)
        return m, l, alpha

    m, l, alpha = lax.fori_loop(0, (n_kv - 2) // 2, body, (m, l, alpha))
    m, l, alpha_n = softmax(s_b, p_b, m, l)
    weighted(n_kv - 2, p_a, alpha)
    weighted(n_kv - 1, p_b, alpha_n)
    o_t = acc_ref[...] / l
    for g in range(group):
        o_ref[0, :, g * dv:(g + 1) * dv] = o_t[:, g * tq:(g + 1) * tq].T.astype(o_ref.dtype)


def _attention(qT, k, vT, *, tq):
    B, Hq, dk, S = qT.shape
    _, Hkv, n_kv, dv, tc = vT.shape
    group = Hq // Hkv
    width = group * tq
    assert n_kv >= 2 and n_kv % 2 == 0
    return pl.pallas_call(
        functools.partial(_attn_kernel, group=group, n_kv=n_kv),
        grid=(B, Hkv, S // tq),
        in_specs=[
            pl.BlockSpec((1, group, dk, tq), lambda b, h, q: (b, h, 0, q)),
            pl.BlockSpec((1, 1, S, dk), lambda b, h, q: (b, h, 0, 0)),
            pl.BlockSpec((1, 1, n_kv, dv, tc), lambda b, h, q: (b, h, 0, 0, 0)),
        ],
        out_specs=pl.BlockSpec((1, tq, group * dv), lambda b, h, q: (b, q, h)),
        out_shape=jax.ShapeDtypeStruct((B, S, Hq * dv), _BF16),
        scratch_shapes=[pltpu.VMEM((tc, width), _F32), pltpu.VMEM((tc, width), _F32),
                        pltpu.VMEM((tc, width), _BF16), pltpu.VMEM((tc, width), _BF16),
                        pltpu.VMEM((dv, width), _F32)],
        compiler_params=pltpu.CompilerParams(
            dimension_semantics=("parallel", "parallel", "parallel"),
            vmem_limit_bytes=VMEM_LIMIT_BYTES),
        name="attn",
    )(qT, k, vT)


def _mlp_kernel(x_ref, oa_ref, ob_ref, wo_a_ref, wo_b_ref, g_mlp_ref, w_up_ref, w_dn_ref, g_fin_ref,
                y_ref, *, final):
    x1 = (x_ref[...]
          + jnp.dot(oa_ref[...], wo_a_ref[...], preferred_element_type=_F32)
          + jnp.dot(ob_ref[...], wo_b_ref[...], preferred_element_type=_F32))
    hn = _rms(x1, g_mlp_ref[...]).astype(_BF16)
    y_ref[...] = x1
    for c in range(D_FF // FF_CHUNK):
        u = jnp.dot(hn, w_up_ref[:, c * FF_CHUNK:(c + 1) * FF_CHUNK], preferred_element_type=_F32)
        a = jnp.square(jnp.maximum(u, 0.0)).astype(_BF16)
        y_ref[...] += jnp.dot(a, w_dn_ref[c * FF_CHUNK:(c + 1) * FF_CHUNK, :], preferred_element_type=_F32)
    if final:
        y_ref[...] = _rms(y_ref[...], g_fin_ref[...])


def _single_buffered(shape):
    nd = len(shape)
    return pl.BlockSpec(shape, lambda *_: (0,) * nd, pipeline_mode=pl.Buffered(1))


def _mlp(x, oa, ob, wo_a, wo_b, g_mlp, w_up, w_dn, g_fin, *, final):
    B, S, D = x.shape
    tm = MLP_TM
    tok = lambda w: pl.BlockSpec((None, tm, w), lambda b, s: (b, s, 0))
    return pl.pallas_call(
        functools.partial(_mlp_kernel, final=final),
        grid=(B, S // tm),
        in_specs=[
            tok(D), tok(oa.shape[-1]), tok(ob.shape[-1]),
            _single_buffered(wo_a.shape), _single_buffered(wo_b.shape), _const_spec(g_mlp.shape),
            _single_buffered(w_up.shape), _single_buffered(w_dn.shape), _const_spec(g_fin.shape),
        ],
        out_specs=tok(D),
        out_shape=jax.ShapeDtypeStruct((B, S, D), _F32),
        compiler_params=pltpu.CompilerParams(
            dimension_semantics=("parallel", "parallel"), vmem_limit_bytes=VMEM_LIMIT_BYTES),
        name="mlp",
    )(x, oa, ob, wo_a, wo_b, g_mlp, w_up, w_dn, g_fin)


def _rotary_tables(S):
    t = jnp.arange(S)
    row = (t // GRID_W).astype(_F32)[:, None]
    col = (t % GRID_W).astype(_F32)[:, None]
    lane = jnp.arange(LANES)

    def tables(group):
        half = group // 2
        quarter = half // 2
        j = lane % quarter
        freqs = ROPE_THETA ** (-(2.0 * j.astype(_F32)) / half)
        pos = jnp.where((lane % group) < half, row, col)
        ang = pos * freqs[None, :]
        first = (lane % half) < quarter
        cos = jnp.cos(ang)
        sin = jnp.sin(ang)
        return cos, jnp.where(first, -sin, 0.0), jnp.where(first, 0.0, sin)

    return jnp.concatenate(tables(GQA_HEAD_DIM) + tables(MLA_ROPE), axis=1)


def _relayout_weights(w_in, w_mla_q_up):
    o = [0]
    for w in (MLA_Q_LORA, MLA_KV_LORA, MLA_ROPE, GQA_HEADS * GQA_HEAD_DIM,
              GQA_KV_HEADS * GQA_HEAD_DIM, GQA_KV_HEADS * GQA_HEAD_DIM):
        o.append(o[-1] + w)
    cq, ckv, kr, qb, kb, vb = (w_in[..., o[i]:o[i + 1]] for i in range(6))
    pad = jnp.zeros(w_in.shape[:-1] + (LANES - MLA_ROPE,), w_in.dtype)
    w_in_p = jnp.concatenate([cq, ckv, qb, kb, vb, kr, pad], axis=-1).astype(_BF16)
    L, R, _ = w_mla_q_up.shape
    wq = w_mla_q_up.reshape(L, R, MLA_HEADS, MLA_NOPE + MLA_ROPE)
    wq = jnp.pad(wq, ((0, 0), (0, 0), (0, 0), (0, MLA_DK - MLA_NOPE - MLA_ROPE)))
    return w_in_p, wq.reshape(L, R, MLA_HEADS * MLA_DK).astype(_BF16)


def _trunk(x, p, tab):
    for l in range(DEPTH):
        qaT, ka, vaT, qbT, kb, vbT = _project(
            x, p["attn_norm"][l], p["w_in"][l], p["mla_q_norm"][l], p["w_qup"][l],
            p["mla_kv_norm"][l], p["w_kvup"][l], p["gqa_q_norm"][l], p["gqa_k_norm"][l], tab)
        oa = _attention(qaT, ka, vaT, tq=MLA_TQ)
        ob = _attention(qbT, kb, vbT, tq=GQA_TQ)
        x = _mlp(x, oa, ob, p["wo_a"][l], p["wo_b"][l], p["mlp_norm"][l], p["w_up"][l], p["w_dn"][l],
                 p["final_norm"], final=(l == DEPTH - 1))
    return x


def kernel(x_prompt, x_sample, attn_norm, w_in, mla_q_norm, w_mla_q_up, mla_kv_norm, w_mla_kv_up,
           gqa_q_norm, gqa_k_norm, w_out, mlp_norm, w_mlp_up, w_mlp_down, final_norm):
    w_in_p, w_qup = _relayout_weights(w_in, w_mla_q_up)
    row = lambda g: g[:, None, :]
    split = MLA_HEADS * MLA_V
    p = {
        "attn_norm": row(attn_norm), "w_in": w_in_p, "mla_q_norm": row(mla_q_norm), "w_qup": w_qup,
        "mla_kv_norm": row(mla_kv_norm), "w_kvup": w_mla_kv_up.astype(_BF16),
        "gqa_q_norm": row(gqa_q_norm), "gqa_k_norm": row(gqa_k_norm),
        "wo_a": w_out[:, :split, :].astype(_BF16), "wo_b": w_out[:, split:, :].astype(_BF16),
        "mlp_norm": row(mlp_norm), "w_up": w_mlp_up.astype(_BF16), "w_dn": w_mlp_down.astype(_BF16),
        "final_norm": final_norm[None, :],
    }
    tab = _rotary_tables(max(x_prompt.shape[1], x_sample.shape[1]))
    return _trunk(x_prompt, p, tab), _trunk(x_sample, p, tab)
```

```python
import functools
import math

import jax
import jax.numpy as jnp
from jax import lax
from jax.experimental import pallas as pl
from jax.experimental.pallas import tpu as pltpu

D_MODEL = 1024
GRID_W = 64
ROPE_THETA = 10000.0
NORM_EPS = 1e-6
MLA_HEADS = 4
MLA_Q_LORA = 384
MLA_KV_LORA = 256
MLA_NOPE = 128
MLA_ROPE = 64
MLA_V = 128
GQA_HEADS = 4
GQA_KV_HEADS = 2
GQA_HEAD_DIM = 128
D_FF = 4 * D_MODEL
DEPTH = 2

LANES = 128
MXU_DIM = 256
VMEM_LIMIT_BYTES = 56 * 1024 * 1024

PROJ_TM = 512
MLP_TM = 512
KV_CHUNK = 512
ATTN_PROBLEMS = 2
MLA_TQ = 1024
GQA_TQ = 512
FF_CHUNK = 1024

MLA_DK = 2 * LANES
IN_WIDTH_PADDED = 1792
LOG2E = math.log2(math.e)
NEG_BIG = -1e30

_BF16 = jnp.bfloat16
_F32 = jnp.float32


def _rms(x, g):
    ms = jnp.mean(x * x, axis=-1, keepdims=True)
    return x * lax.rsqrt(ms + NORM_EPS) * g


def _rotary(x, cos, sin_lo, sin_hi, half):
    return x * cos + pltpu.roll(x, LANES - half, 1) * sin_lo + pltpu.roll(x, half, 1) * sin_hi


def _proj_kernel(x_ref, g_attn_ref, w_in_ref, g_q_ref, w_qup_ref, g_kv_ref, w_kvup_ref,
                 g_gq_ref, g_gk_ref, tab_ref,
                 qaT_ref, ka_ref, vaT_ref, qbT_ref, kb_ref, vbT_ref, *, n_chunks):
    tc = KV_CHUNK
    h = _rms(x_ref[...], g_attn_ref[...])
    z = jnp.dot(h.astype(_BF16), w_in_ref[...], preferred_element_type=_F32)
    o_ckv = MLA_Q_LORA
    o_qb = o_ckv + MLA_KV_LORA
    o_kb = o_qb + GQA_HEADS * GQA_HEAD_DIM
    o_vb = o_kb + GQA_KV_HEADS * GQA_HEAD_DIM
    o_kr = o_vb + GQA_KV_HEADS * GQA_HEAD_DIM
    cq = z[:, :o_ckv]
    ckv = z[:, o_ckv:o_qb]

    tab = tab_ref[...]
    g_cos, g_slo, g_shi = tab[:, 0:128], tab[:, 128:256], tab[:, 256:384]
    m_cos, m_slo, m_shi = tab[:, 384:512], tab[:, 512:640], tab[:, 640:768]
    rope_g = functools.partial(_rotary, cos=g_cos, sin_lo=g_slo, sin_hi=g_shi, half=GQA_HEAD_DIM // 4)
    rope_m = functools.partial(_rotary, cos=m_cos, sin_lo=m_slo, sin_hi=m_shi, half=MLA_ROPE // 4)

    scale_a = (MLA_NOPE + MLA_ROPE) ** -0.5 * LOG2E
    scale_b = GQA_HEAD_DIM ** -0.5 * LOG2E

    def store_t(ref, head, val):
        for c in range(n_chunks):
            ref[0, head, c] = val[c * tc:(c + 1) * tc, :].T.astype(_BF16)

    qa = jnp.dot(_rms(cq, g_q_ref[...]).astype(_BF16), w_qup_ref[...],
                 preferred_element_type=_F32)
    kva = jnp.dot(_rms(ckv, g_kv_ref[...]).astype(_BF16), w_kvup_ref[...],
                  preferred_element_type=_F32)
    k_rope = rope_m(z[:, o_kr:o_kr + LANES]).astype(_BF16)
    for hd in range(MLA_HEADS):
        base = hd * MLA_DK
        q_nope = qa[:, base:base + LANES] * scale_a
        q_rope = rope_m(qa[:, base + LANES:base + 2 * LANES]) * scale_a
        qaT_ref[0, hd, 0:LANES, :] = q_nope.T.astype(_BF16)
        qaT_ref[0, hd, LANES:2 * LANES, :] = q_rope.T.astype(_BF16)
        ka_ref[0, hd, :, 0:LANES] = kva[:, base:base + LANES].astype(_BF16)
        ka_ref[0, hd, :, LANES:2 * LANES] = k_rope
        store_t(vaT_ref, hd, kva[:, base + LANES:base + 2 * LANES])

    g_gq = g_gq_ref[...]
    g_gk = g_gk_ref[...]
    for hd in range(GQA_HEADS):
        qh = z[:, o_qb + hd * LANES:o_qb + (hd + 1) * LANES]
        qbT_ref[0, hd] = (rope_g(_rms(qh, g_gq)) * scale_b).T.astype(_BF16)
    for hd in range(GQA_KV_HEADS):
        kh = z[:, o_kb + hd * LANES:o_kb + (hd + 1) * LANES]
        kb_ref[0, hd] = rope_g(_rms(kh, g_gk)).astype(_BF16)
        store_t(vbT_ref, hd, z[:, o_vb + hd * LANES:o_vb + (hd + 1) * LANES])


def _const_spec(shape):
    nd = len(shape)
    return pl.BlockSpec(shape, lambda *_: (0,) * nd)


def _project(x, g_attn, w_in, g_q, w_qup, g_kv, w_kvup, g_gq, g_gk, tab):
    B, S, D = x.shape
    tm = PROJ_TM
    n_chunks = tm // KV_CHUNK
    ns = S // tm
    nc = S // KV_CHUNK
    out_shape = (
        jax.ShapeDtypeStruct((B, MLA_HEADS, MLA_DK, S), _BF16),
        jax.ShapeDtypeStruct((B, MLA_HEADS, S, MLA_DK), _BF16),
        jax.ShapeDtypeStruct((B, MLA_HEADS, nc, MLA_V, KV_CHUNK), _BF16),
        jax.ShapeDtypeStruct((B, GQA_HEADS, GQA_HEAD_DIM, S), _BF16),
        jax.ShapeDtypeStruct((B, GQA_KV_HEADS, S, GQA_HEAD_DIM), _BF16),
        jax.ShapeDtypeStruct((B, GQA_KV_HEADS, nc, GQA_HEAD_DIM, KV_CHUNK), _BF16),
    )
    in_specs = [
        pl.BlockSpec((None, tm, D), lambda b, s: (b, s, 0)),
        _const_spec(g_attn.shape), _const_spec(w_in.shape), _const_spec(g_q.shape),
        _const_spec(w_qup.shape), _const_spec(g_kv.shape), _const_spec(w_kvup.shape),
        _const_spec(g_gq.shape), _const_spec(g_gk.shape),
        pl.BlockSpec((tm, tab.shape[1]), lambda b, s: (s, 0)),
    ]
    out_specs = (
        pl.BlockSpec((1, MLA_HEADS, MLA_DK, tm), lambda b, s: (b, 0, 0, s)),
        pl.BlockSpec((1, MLA_HEADS, tm, MLA_DK), lambda b, s: (b, 0, s, 0)),
        pl.BlockSpec((1, MLA_HEADS, n_chunks, MLA_V, KV_CHUNK), lambda b, s: (b, 0, s, 0, 0)),
        pl.BlockSpec((1, GQA_HEADS, GQA_HEAD_DIM, tm), lambda b, s: (b, 0, 0, s)),
        pl.BlockSpec((1, GQA_KV_HEADS, tm, GQA_HEAD_DIM), lambda b, s: (b, 0, s, 0)),
        pl.BlockSpec((1, GQA_KV_HEADS, n_chunks, GQA_HEAD_DIM, KV_CHUNK), lambda b, s: (b, 0, s, 0, 0)),
    )
    return pl.pallas_call(
        functools.partial(_proj_kernel, n_chunks=n_chunks),
        grid=(B, ns),
        in_specs=in_specs,
        out_specs=out_specs,
        out_shape=out_shape,
        compiler_params=pltpu.CompilerParams(
            dimension_semantics=("parallel", "parallel"), vmem_limit_bytes=VMEM_LIMIT_BYTES),
        name="proj",
    )(x, g_attn, w_in, g_q, w_qup, g_kv, w_kvup, g_gq, g_gk, tab)


def _attn_kernel(qT_ref, k_ref, vT_ref, o_ref, *scratch, group, n_kv):
    tc = KV_CHUNK
    sub = tc // 2
    dv = vT_ref.shape[3]
    n_prob = len(scratch) // 3
    tqp = qT_ref.shape[3] // n_prob
    width = group * tqp
    probs = [scratch[3 * i:3 * i + 3] for i in range(n_prob)]
    q_ts = [jnp.concatenate([qT_ref[0, g, :, i * tqp:(i + 1) * tqp] for g in range(group)], axis=1)
            for i in range(n_prob)]

    def score_dots(j):
        off = pl.multiple_of(j * tc, tc)
        k_sub = [k_ref[0, 0, pl.ds(off + r * sub, sub), :] for r in range(2)]
        return [[jnp.dot(kr, q_t, preferred_element_type=_F32) for kr in k_sub] for q_t in q_ts]

    def store_scores(s_ref, r, d):
        s_ref[r * sub:(r + 1) * sub, :] = d
        return jnp.max(d, axis=0, keepdims=True)

    def exp_block(s_ref, p_ref, r, m_new):
        p = jnp.exp2(s_ref[r * sub:(r + 1) * sub, :] - m_new)
        p_ref[r * sub:(r + 1) * sub, :] = p.astype(_BF16)
        return jnp.sum(p, axis=0, keepdims=True)

    def step(j, carries, *, has_pv, has_scores):
        dots = score_dots(j + 1) if has_scores else None
        if has_pv:
            v_blk = vT_ref[0, 0, j - 1]
            pvs = [jnp.dot(v_blk, p_ref[...], preferred_element_type=_F32) for _, p_ref, _ in probs]
        out = []
        for i, (s_ref, p_ref, acc_ref) in enumerate(probs):
            m, l, alpha_prev, c_cur = carries[i]
            m_new = jnp.maximum(m, c_cur)
            alpha = jnp.exp2(m - m_new)
            l_new = alpha * l
            c_nxt = None
            for r in range(2):
                l_new = l_new + exp_block(s_ref, p_ref, r, m_new)
                if has_scores:
                    c_r = store_scores(s_ref, r, dots[i][r])
                    c_nxt = c_r if c_nxt is None else jnp.maximum(c_nxt, c_r)
            if has_pv:
                acc_ref[...] = alpha_prev * acc_ref[...] + pvs[i]
            out.append((m_new, l_new, alpha, c_nxt))
        return out

    carries = []
    for (s_ref, _, acc_ref), d in zip(probs, score_dots(0)):
        acc_ref[...] = jnp.zeros_like(acc_ref)
        c0 = jnp.maximum(store_scores(s_ref, 0, d[0]), store_scores(s_ref, 1, d[1]))
        carries.append((jnp.full((1, width), NEG_BIG, _F32), jnp.zeros((1, width), _F32),
                        jnp.ones((1, width), _F32), c0))
    carries = step(0, carries, has_pv=False, has_scores=True)
    carries = lax.fori_loop(1, n_kv - 1,
                            lambda j, c: step(j, c, has_pv=True, has_scores=True), carries)
    carries = step(n_kv - 1, carries, has_pv=True, has_scores=False)
    v_blk = vT_ref[0, 0, n_kv - 1]
    for i, (_, p_ref, acc_ref) in enumerate(probs):
        _, l, alpha, _ = carries[i]
        acc = alpha * acc_ref[...] + jnp.dot(v_blk, p_ref[...], preferred_element_type=_F32)
        o_t = acc / l
        for g in range(group):
            o_ref[0, i * tqp:(i + 1) * tqp, g * dv:(g + 1) * dv] = (
                o_t[:, g * tqp:(g + 1) * tqp].T.astype(o_ref.dtype))


def _attention(qT, k, vT, *, tq):
    B, Hq, dk, S = qT.shape
    _, Hkv, n_kv, dv, tc = vT.shape
    group = Hq // Hkv
    width = group * tq // ATTN_PROBLEMS
    assert n_kv >= 2
    return pl.pallas_call(
        functools.partial(_attn_kernel, group=group, n_kv=n_kv),
        grid=(B, Hkv, S // tq),
        in_specs=[
            pl.BlockSpec((1, group, dk, tq), lambda b, h, q: (b, h, 0, q)),
            pl.BlockSpec((1, 1, S, dk), lambda b, h, q: (b, h, 0, 0)),
            pl.BlockSpec((1, 1, n_kv, dv, tc), lambda b, h, q: (b, h, 0, 0, 0)),
        ],
        out_specs=pl.BlockSpec((1, tq, group * dv), lambda b, h, q: (b, q, h)),
        out_shape=jax.ShapeDtypeStruct((B, S, Hq * dv), _BF16),
        scratch_shapes=[pltpu.VMEM((tc, width), _F32), pltpu.VMEM((tc, width), _BF16),
                        pltpu.VMEM((dv, width), _F32)] * ATTN_PROBLEMS,
        compiler_params=pltpu.CompilerParams(
            dimension_semantics=("parallel", "parallel", "parallel"),
            vmem_limit_bytes=VMEM_LIMIT_BYTES),
        name="attn",
    )(qT, k, vT)


def _mlp_kernel(x_ref, oa_ref, ob_ref, wo_a_ref, wo_b_ref, g_mlp_ref, w_up_ref, w_dn_ref, g_fin_ref,
                y_ref, *, final):
    x1 = (x_ref[...]
          + jnp.dot(oa_ref[...], wo_a_ref[...], preferred_element_type=_F32)
          + jnp.dot(ob_ref[...], wo_b_ref[...], preferred_element_type=_F32))
    hn = _rms(x1, g_mlp_ref[...]).astype(_BF16)
    y_ref[...] = x1
    for c in range(D_FF // FF_CHUNK):
        u = jnp.dot(hn, w_up_ref[:, c * FF_CHUNK:(c + 1) * FF_CHUNK], preferred_element_type=_F32)
        a = jnp.square(jnp.maximum(u, 0.0)).astype(_BF16)
        y_ref[...] += jnp.dot(a, w_dn_ref[c * FF_CHUNK:(c + 1) * FF_CHUNK, :], preferred_element_type=_F32)
    if final:
        y_ref[...] = _rms(y_ref[...], g_fin_ref[...])


def _single_buffered(shape):
    nd = len(shape)
    return pl.BlockSpec(shape, lambda *_: (0,) * nd, pipeline_mode=pl.Buffered(1))


def _mlp(x, oa, ob, wo_a, wo_b, g_mlp, w_up, w_dn, g_fin, *, final):
    B, S, D = x.shape
    tm = MLP_TM
    tok = lambda w: pl.BlockSpec((None, tm, w), lambda b, s: (b, s, 0))
    return pl.pallas_call(
        functools.partial(_mlp_kernel, final=final),
        grid=(B, S // tm),
        in_specs=[
            tok(D), tok(oa.shape[-1]), tok(ob.shape[-1]),
            _single_buffered(wo_a.shape), _single_buffered(wo_b.shape), _const_spec(g_mlp.shape),
            _single_buffered(w_up.shape), _single_buffered(w_dn.shape), _const_spec(g_fin.shape),
        ],
        out_specs=tok(D),
        out_shape=jax.ShapeDtypeStruct((B, S, D), _F32),
        compiler_params=pltpu.CompilerParams(
            dimension_semantics=("parallel", "parallel"), vmem_limit_bytes=VMEM_LIMIT_BYTES),
        name="mlp",
    )(x, oa, ob, wo_a, wo_b, g_mlp, w_up, w_dn, g_fin)


def _rotary_tables(S):
    t = jnp.arange(S)
    row = (t // GRID_W).astype(_F32)[:, None]
    col = (t % GRID_W).astype(_F32)[:, None]
    lane = jnp.arange(LANES)

    def tables(group):
        half = group // 2
        quarter = half // 2
        j = lane % quarter
        freqs = ROPE_THETA ** (-(2.0 * j.astype(_F32)) / half)
        pos = jnp.where((lane % group) < half, row, col)
        ang = pos * freqs[None, :]
        first = (lane % half) < quarter
        cos = jnp.cos(ang)
        sin = jnp.sin(ang)
        return cos, jnp.where(first, -sin, 0.0), jnp.where(first, 0.0, sin)

    return jnp.concatenate(tables(GQA_HEAD_DIM) + tables(MLA_ROPE), axis=1)


def _relayout_weights(w_in, w_mla_q_up):
    o = [0]
    for w in (MLA_Q_LORA, MLA_KV_LORA, MLA_ROPE, GQA_HEADS * GQA_HEAD_DIM,
              GQA_KV_HEADS * GQA_HEAD_DIM, GQA_KV_HEADS * GQA_HEAD_DIM):
        o.append(o[-1] + w)
    cq, ckv, kr, qb, kb, vb = (w_in[..., o[i]:o[i + 1]] for i in range(6))
    pad = jnp.zeros(w_in.shape[:-1] + (LANES - MLA_ROPE,), w_in.dtype)
    w_in_p = jnp.concatenate([cq, ckv, qb, kb, vb, kr, pad], axis=-1).astype(_BF16)
    L, R, _ = w_mla_q_up.shape
    wq = w_mla_q_up.reshape(L, R, MLA_HEADS, MLA_NOPE + MLA_ROPE)
    wq = jnp.pad(wq, ((0, 0), (0, 0), (0, 0), (0, MLA_DK - MLA_NOPE - MLA_ROPE)))
    return w_in_p, wq.reshape(L, R, MLA_HEADS * MLA_DK).astype(_BF16)


def _trunk(x, p, tab):
    for l in range(DEPTH):
        qaT, ka, vaT, qbT, kb, vbT = _project(
            x, p["attn_norm"][l], p["w_in"][l], p["mla_q_norm"][l], p["w_qup"][l],
            p["mla_kv_norm"][l], p["w_kvup"][l], p["gqa_q_norm"][l], p["gqa_k_norm"][l], tab)
        oa = _attention(qaT, ka, vaT, tq=MLA_TQ)
        ob = _attention(qbT, kb, vbT, tq=GQA_TQ)
        x = _mlp(x, oa, ob, p["wo_a"][l], p["wo_b"][l], p["mlp_norm"][l], p["w_up"][l], p["w_dn"][l],
                 p["final_norm"], final=(l == DEPTH - 1))
    return x


def kernel(x_prompt, x_sample, attn_norm, w_in, mla_q_norm, w_mla_q_up, mla_kv_norm, w_mla_kv_up,
           gqa_q_norm, gqa_k_norm, w_out, mlp_norm, w_mlp_up, w_mlp_down, final_norm):
    w_in_p, w_qup = _relayout_weights(w_in, w_mla_q_up)
    row = lambda g: g[:, None, :]
    split = MLA_HEADS * MLA_V
    p = {
        "attn_norm": row(attn_norm), "w_in": w_in_p, "mla_q_norm": row(mla_q_norm), "w_qup": w_qup,
        "mla_kv_norm": row(mla_kv_norm), "w_kvup": w_mla_kv_up.astype(_BF16),
        "gqa_q_norm": row(gqa_q_norm), "gqa_k_norm": row(gqa_k_norm),
        "wo_a": w_out[:, :split, :].astype(_BF16), "wo_b": w_out[:, split:, :].astype(_BF16),
        "mlp_norm": row(mlp_norm), "w_up": w_mlp_up.astype(_BF16), "w_dn": w_mlp_down.astype(_BF16),
        "final_norm": final_norm[None, :],
    }
    tab = _rotary_tables(max(x_prompt.shape[1], x_sample.shape[1]))
    return _trunk(x_prompt, p, tab), _trunk(x_sample, p, tab)
```

```python
import functools
import math

import jax
import jax.numpy as jnp
from jax import lax
from jax.experimental import pallas as pl
from jax.experimental.pallas import tpu as pltpu

D_MODEL = 1024
GRID_W = 64
ROPE_THETA = 10000.0
NORM_EPS = 1e-6
MLA_HEADS = 4
MLA_Q_LORA = 384
MLA_KV_LORA = 256
MLA_NOPE = 128
MLA_ROPE = 64
MLA_V = 128
GQA_HEADS = 4
GQA_KV_HEADS = 2
GQA_HEAD_DIM = 128
D_FF = 4 * D_MODEL
DEPTH = 2

LANES = 128
MXU_DIM = 256
VMEM_LIMIT_BYTES = 56 * 1024 * 1024

PROJ_TM = 512
MLP_TM = 512
KV_CHUNK = 512
ATTN_PROBLEMS = 4
MLA_TQ = 2048
GQA_TQ = 1024
ATTN_UNROLL = 2
FF_CHUNK = 1024

MLA_DK = 2 * LANES
V_EXTRA_ROWS = 16
IN_WIDTH_PADDED = 1792
LOG2E = math.log2(math.e)
NEG_BIG = -1e30

_BF16 = jnp.bfloat16
_F32 = jnp.float32


def _rms(x, g):
    ms = jnp.mean(x * x, axis=-1, keepdims=True)
    return x * lax.rsqrt(ms + NORM_EPS) * g


def _rotary(x, cos, sin_lo, sin_hi, half):
    return x * cos + pltpu.roll(x, LANES - half, 1) * sin_lo + pltpu.roll(x, half, 1) * sin_hi


def _proj_kernel(x_ref, g_attn_ref, w_in_ref, g_q_ref, w_qup_ref, g_kv_ref, w_kvup_ref,
                 g_gq_ref, g_gk_ref, tab_ref,
                 qaT_ref, ka_ref, vaT_ref, qbT_ref, kb_ref, vbT_ref, *, n_chunks):
    tc = KV_CHUNK
    h = _rms(x_ref[...], g_attn_ref[...])
    z = jnp.dot(h.astype(_BF16), w_in_ref[...], preferred_element_type=_F32)
    o_ckv = MLA_Q_LORA
    o_qb = o_ckv + MLA_KV_LORA
    o_kb = o_qb + GQA_HEADS * GQA_HEAD_DIM
    o_vb = o_kb + GQA_KV_HEADS * GQA_HEAD_DIM
    o_kr = o_vb + GQA_KV_HEADS * GQA_HEAD_DIM
    cq = z[:, :o_ckv]
    ckv = z[:, o_ckv:o_qb]

    tab = tab_ref[...]
    g_cos, g_slo, g_shi = tab[:, 0:128], tab[:, 128:256], tab[:, 256:384]
    m_cos, m_slo, m_shi = tab[:, 384:512], tab[:, 512:640], tab[:, 640:768]
    rope_g = functools.partial(_rotary, cos=g_cos, sin_lo=g_slo, sin_hi=g_shi, half=GQA_HEAD_DIM // 4)
    rope_m = functools.partial(_rotary, cos=m_cos, sin_lo=m_slo, sin_hi=m_shi, half=MLA_ROPE // 4)

    scale_a = (MLA_NOPE + MLA_ROPE) ** -0.5 * LOG2E
    scale_b = GQA_HEAD_DIM ** -0.5 * LOG2E

    ones_rows = jnp.where(lax.broadcasted_iota(jnp.int32, (V_EXTRA_ROWS, tc), 0) == 0,
                          1.0, 0.0).astype(_BF16)

    def store_t(ref, head, val):
        dv = val.shape[1]
        for c in range(n_chunks):
            ref[0, head, c, 0:dv, :] = val[c * tc:(c + 1) * tc, :].T.astype(_BF16)
            ref[0, head, c, dv:dv + V_EXTRA_ROWS, :] = ones_rows

    qa = jnp.dot(_rms(cq, g_q_ref[...]).astype(_BF16), w_qup_ref[...],
                 preferred_element_type=_F32)
    kva = jnp.dot(_rms(ckv, g_kv_ref[...]).astype(_BF16), w_kvup_ref[...],
                  preferred_element_type=_F32)
    k_rope = rope_m(z[:, o_kr:o_kr + LANES]).astype(_BF16)
    for hd in range(MLA_HEADS):
        base = hd * MLA_DK
        q_nope = qa[:, base:base + LANES] * scale_a
        q_rope = rope_m(qa[:, base + LANES:base + 2 * LANES]) * scale_a
        qaT_ref[0, hd, 0:LANES, :] = q_nope.T.astype(_BF16)
        qaT_ref[0, hd, LANES:2 * LANES, :] = q_rope.T.astype(_BF16)
        ka_ref[0, hd, :, 0:LANES] = kva[:, base:base + LANES].astype(_BF16)
        ka_ref[0, hd, :, LANES:2 * LANES] = k_rope
        store_t(vaT_ref, hd, kva[:, base + LANES:base + 2 * LANES])

    g_gq = g_gq_ref[...]
    g_gk = g_gk_ref[...]
    for hd in range(GQA_HEADS):
        qh = z[:, o_qb + hd * LANES:o_qb + (hd + 1) * LANES]
        qbT_ref[0, hd] = (rope_g(_rms(qh, g_gq)) * scale_b).T.astype(_BF16)
    for hd in range(GQA_KV_HEADS):
        kh = z[:, o_kb + hd * LANES:o_kb + (hd + 1) * LANES]
        kb_ref[0, hd] = rope_g(_rms(kh, g_gk)).astype(_BF16)
        store_t(vbT_ref, hd, z[:, o_vb + hd * LANES:o_vb + (hd + 1) * LANES])


def _const_spec(shape):
    nd = len(shape)
    return pl.BlockSpec(shape, lambda *_: (0,) * nd)


def _project(x, g_attn, w_in, g_q, w_qup, g_kv, w_kvup, g_gq, g_gk, tab):
    B, S, D = x.shape
    tm = PROJ_TM
    n_chunks = tm // KV_CHUNK
    ns = S // tm
    nc = S // KV_CHUNK
    out_shape = (
        jax.ShapeDtypeStruct((B, MLA_HEADS, MLA_DK, S), _BF16),
        jax.ShapeDtypeStruct((B, MLA_HEADS, S, MLA_DK), _BF16),
        jax.ShapeDtypeStruct((B, MLA_HEADS, nc, MLA_V + V_EXTRA_ROWS, KV_CHUNK), _BF16),
        jax.ShapeDtypeStruct((B, GQA_HEADS, GQA_HEAD_DIM, S), _BF16),
        jax.ShapeDtypeStruct((B, GQA_KV_HEADS, S, GQA_HEAD_DIM), _BF16),
        jax.ShapeDtypeStruct((B, GQA_KV_HEADS, nc, GQA_HEAD_DIM + V_EXTRA_ROWS, KV_CHUNK), _BF16),
    )
    in_specs = [
        pl.BlockSpec((None, tm, D), lambda b, s: (b, s, 0)),
        _const_spec(g_attn.shape), _const_spec(w_in.shape), _const_spec(g_q.shape),
        _const_spec(w_qup.shape), _const_spec(g_kv.shape), _const_spec(w_kvup.shape),
        _const_spec(g_gq.shape), _const_spec(g_gk.shape),
        pl.BlockSpec((tm, tab.shape[1]), lambda b, s: (s, 0)),
    ]
    out_specs = (
        pl.BlockSpec((1, MLA_HEADS, MLA_DK, tm), lambda b, s: (b, 0, 0, s)),
        pl.BlockSpec((1, MLA_HEADS, tm, MLA_DK), lambda b, s: (b, 0, s, 0)),
        pl.BlockSpec((1, MLA_HEADS, n_chunks, MLA_V + V_EXTRA_ROWS, KV_CHUNK), lambda b, s: (b, 0, s, 0, 0)),
        pl.BlockSpec((1, GQA_HEADS, GQA_HEAD_DIM, tm), lambda b, s: (b, 0, 0, s)),
        pl.BlockSpec((1, GQA_KV_HEADS, tm, GQA_HEAD_DIM), lambda b, s: (b, 0, s, 0)),
        pl.BlockSpec((1, GQA_KV_HEADS, n_chunks, GQA_HEAD_DIM + V_EXTRA_ROWS, KV_CHUNK),
                     lambda b, s: (b, 0, s, 0, 0)),
    )
    return pl.pallas_call(
        functools.partial(_proj_kernel, n_chunks=n_chunks),
        grid=(B, ns),
        in_specs=in_specs,
        out_specs=out_specs,
        out_shape=out_shape,
        compiler_params=pltpu.CompilerParams(
            dimension_semantics=("parallel", "parallel"), vmem_limit_bytes=VMEM_LIMIT_BYTES),
        name="proj",
    )(x, g_attn, w_in, g_q, w_qup, g_kv, w_kvup, g_gq, g_gk, tab)


def _attn_kernel(qT_ref, k_ref, vT_ref, o_ref, *scratch, group, n_kv, unroll):
    tc = KV_CHUNK
    sub = tc // 2
    dv = vT_ref.shape[3] - V_EXTRA_ROWS
    n_prob = len(scratch) // 5
    tqp = qT_ref.shape[3] // n_prob
    width = group * tqp
    probs = [scratch[5 * i:5 * i + 5] for i in range(n_prob)]
    q_ts = [jnp.concatenate([qT_ref[0, g, :, i * tqp:(i + 1) * tqp] for g in range(group)], axis=1)
            for i in range(n_prob)]

    def k_rows(j):
        off = pl.multiple_of(j * tc, tc)
        return [k_ref[0, 0, pl.ds(off + r * sub, sub), :] for r in range(2)]

    def store_scores(s_ref, r, d):
        s_ref[r * sub:(r + 1) * sub, :] = d
        return jnp.max(d, axis=0, keepdims=True)

    def exp_block(s_ref, p_ref, r, m_new):
        x = (s_ref[r * sub:(r + 1) * sub, :] - m_new).astype(_BF16)
        p_ref[r * sub:(r + 1) * sub, :] = jnp.exp2(x)

    def flush(pending, c):
        for fn in pending:
            c = fn(c)
        return c

    def step(j, parity, carries, pendings, *, has_pv, has_scores):
        k_sub = k_rows(j + 1) if has_scores else None
        v_blk = vT_ref[0, 0, j - 1] if has_pv else None
        issued = []
        for i, (s_a, s_b, p_a, p_b, _) in enumerate(probs):
            p_prev = (p_a, p_b)[1 - parity]
            dots = ([jnp.dot(kr, q_ts[i], preferred_element_type=_F32) for kr in k_sub]
                    if has_scores else None)
            pv = jnp.dot(v_blk, p_prev[...], preferred_element_type=_F32) if has_pv else None
            issued.append((dots, pv))
        new_carries, new_pendings = [], []
        for i, (s_a, s_b, p_a, p_b, acc_ref) in enumerate(probs):
            m, alpha_prev, c_cur = carries[i]
            dots, pv = issued[i]
            s_cur, p_cur, s_nxt = (s_a, s_b)[parity], (p_a, p_b)[parity], (s_a, s_b)[1 - parity]
            c_cur = flush(pendings[i], c_cur)
            m_new = jnp.maximum(m, c_cur)
            alpha = jnp.exp2(m - m_new)
            exp_block(s_cur, p_cur, 0, m_new)
            c_nxt = store_scores(s_nxt, 0, dots[0]) if has_scores else None
            exp_block(s_cur, p_cur, 1, m_new)
            pending = []
            if has_scores:
                pending.append(lambda c, d=dots[1], ref=s_nxt: jnp.maximum(c, store_scores(ref, 1, d)))
            if has_pv:
                def update_acc(c, pv=pv, a=alpha_prev, acc_ref=acc_ref):
                    acc_ref[...] = a * acc_ref[...] + pv
                    return c
                pending.append(update_acc)
            new_carries.append((m_new, alpha, c_nxt))
            new_pendings.append(pending)
        return new_carries, new_pendings

    def settle(carries, pendings):
        return [c[:2] + (flush(p, c[2]),) for c, p in zip(carries, pendings)]

    carries = []
    k_sub = k_rows(0)
    for i, (s_a, _, _, _, acc_ref) in enumerate(probs):
        acc_ref[...] = jnp.zeros_like(acc_ref)
        d = [jnp.dot(kr, q_ts[i], preferred_element_type=_F32) for kr in k_sub]
        c0 = jnp.maximum(store_scores(s_a, 0, d[0]), store_scores(s_a, 1, d[1]))
        carries.append((jnp.full((1, width), NEG_BIG, _F32), jnp.ones((1, width), _F32), c0))
    no_pending = [[] for _ in probs]
    carries = settle(*step(0, 0, carries, no_pending, has_pv=False, has_scores=True))

    def body(it, carries):
        pendings = no_pending
        for h in range(unroll):
            j = it * unroll + (h + 1)
            carries, pendings = step(j, (h + 1) % 2, carries, pendings, has_pv=True, has_scores=True)
        return settle(carries, pendings)

    carries = lax.fori_loop(0, (n_kv - 2) // unroll, body, carries)
    last = (n_kv - 1) % 2
    carries, pendings = step(n_kv - 1, last, carries, no_pending, has_pv=True, has_scores=False)
    v_blk = vT_ref[0, 0, n_kv - 1]
    for i, (_, _, p_a, p_b, acc_ref) in enumerate(probs):
        flush(pendings[i], None)
        _, alpha, _ = carries[i]
        acc = alpha * acc_ref[...] + jnp.dot(v_blk, (p_a, p_b)[last][...], preferred_element_type=_F32)
        o_t = acc[0:dv, :] / acc[dv:dv + 1, :]
        for g in range(group):
            o_ref[0, i * tqp:(i + 1) * tqp, g * dv:(g + 1) * dv] = (
                o_t[:, g * tqp:(g + 1) * tqp].T.astype(o_ref.dtype))


def _attention(qT, k, vT, *, tq):
    B, Hq, dk, S = qT.shape
    _, Hkv, n_kv, dv_ext, tc = vT.shape
    dv = dv_ext - V_EXTRA_ROWS
    group = Hq // Hkv
    width = group * tq // ATTN_PROBLEMS
    unroll = ATTN_UNROLL
    assert n_kv % 2 == 0 and unroll % 2 == 0 and (n_kv - 2) % unroll == 0
    return pl.pallas_call(
        functools.partial(_attn_kernel, group=group, n_kv=n_kv, unroll=unroll),
        grid=(B, Hkv, S // tq),
        in_specs=[
            pl.BlockSpec((1, group, dk, tq), lambda b, h, q: (b, h, 0, q)),
            pl.BlockSpec((1, 1, S, dk), lambda b, h, q: (b, h, 0, 0)),
            pl.BlockSpec((1, 1, n_kv, dv_ext, tc), lambda b, h, q: (b, h, 0, 0, 0)),
        ],
        out_specs=pl.BlockSpec((1, tq, group * dv), lambda b, h, q: (b, q, h)),
        out_shape=jax.ShapeDtypeStruct((B, S, Hq * dv), _BF16),
        scratch_shapes=[pltpu.VMEM((tc, width), _F32), pltpu.VMEM((tc, width), _F32),
                        pltpu.VMEM((tc, width), _BF16), pltpu.VMEM((tc, width), _BF16),
                        pltpu.VMEM((dv_ext, width), _F32)] * ATTN_PROBLEMS,
        compiler_params=pltpu.CompilerParams(
            dimension_semantics=("parallel", "parallel", "parallel"),
            vmem_limit_bytes=VMEM_LIMIT_BYTES),
        name="attn",
    )(qT, k, vT)


def _mlp_kernel(x_ref, oa_ref, ob_ref, wo_a_ref, wo_b_ref, g_mlp_ref, w_up_ref, w_dn_ref, g_fin_ref,
                y_ref, *, final):
    x1 = (x_ref[...]
          + jnp.dot(oa_ref[...], wo_a_ref[...], preferred_element_type=_F32)
          + jnp.dot(ob_ref[...], wo_b_ref[...], preferred_element_type=_F32))
    hn = _rms(x1, g_mlp_ref[...]).astype(_BF16)
    y_ref[...] = x1
    for c in range(D_FF // FF_CHUNK):
        u = jnp.dot(hn, w_up_ref[:, c * FF_CHUNK:(c + 1) * FF_CHUNK], preferred_element_type=_F32)
        a = jnp.square(jnp.maximum(u, 0.0)).astype(_BF16)
        y_ref[...] += jnp.dot(a, w_dn_ref[c * FF_CHUNK:(c + 1) * FF_CHUNK, :], preferred_element_type=_F32)
    if final:
        y_ref[...] = _rms(y_ref[...], g_fin_ref[...])


def _single_buffered(shape):
    nd = len(shape)
    return pl.BlockSpec(shape, lambda *_: (0,) * nd, pipeline_mode=pl.Buffered(1))


def _mlp(x, oa, ob, wo_a, wo_b, g_mlp, w_up, w_dn, g_fin, *, final):
    B, S, D = x.shape
    tm = MLP_TM
    tok = lambda w: pl.BlockSpec((None, tm, w), lambda b, s: (b, s, 0))
    return pl.pallas_call(
        functools.partial(_mlp_kernel, final=final),
        grid=(B, S // tm),
        in_specs=[
            tok(D), tok(oa.shape[-1]), tok(ob.shape[-1]),
            _single_buffered(wo_a.shape), _single_buffered(wo_b.shape), _const_spec(g_mlp.shape),
            _single_buffered(w_up.shape), _single_buffered(w_dn.shape), _const_spec(g_fin.shape),
        ],
        out_specs=tok(D),
        out_shape=jax.ShapeDtypeStruct((B, S, D), _F32),
        compiler_params=pltpu.CompilerParams(
            dimension_semantics=("parallel", "parallel"), vmem_limit_bytes=VMEM_LIMIT_BYTES),
        name="mlp",
    )(x, oa, ob, wo_a, wo_b, g_mlp, w_up, w_dn, g_fin)


def _rotary_tables(S):
    t = jnp.arange(S)
    row = (t // GRID_W).astype(_F32)[:, None]
    col = (t % GRID_W).astype(_F32)[:, None]
    lane = jnp.arange(LANES)

    def tables(group):
        half = group // 2
        quarter = half // 2
        j = lane % quarter
        freqs = ROPE_THETA ** (-(2.0 * j.astype(_F32)) / half)
        pos = jnp.where((lane % group) < half, row, col)
        ang = pos * freqs[None, :]
        first = (lane % half) < quarter
        cos = jnp.cos(ang)
        sin = jnp.sin(ang)
        return cos, jnp.where(first, -sin, 0.0), jnp.where(first, 0.0, sin)

    return jnp.concatenate(tables(GQA_HEAD_DIM) + tables(MLA_ROPE), axis=1)


def _relayout_weights(w_in, w_mla_q_up):
    o = [0]
    for w in (MLA_Q_LORA, MLA_KV_LORA, MLA_ROPE, GQA_HEADS * GQA_HEAD_DIM,
              GQA_KV_HEADS * GQA_HEAD_DIM, GQA_KV_HEADS * GQA_HEAD_DIM):
        o.append(o[-1] + w)
    cq, ckv, kr, qb, kb, vb = (w_in[..., o[i]:o[i + 1]] for i in range(6))
    pad = jnp.zeros(w_in.shape[:-1] + (LANES - MLA_ROPE,), w_in.dtype)
    w_in_p = jnp.concatenate([cq, ckv, qb, kb, vb, kr, pad], axis=-1).astype(_BF16)
    L, R, _ = w_mla_q_up.shape
    wq = w_mla_q_up.reshape(L, R, MLA_HEADS, MLA_NOPE + MLA_ROPE)
    wq = jnp.pad(wq, ((0, 0), (0, 0), (0, 0), (0, MLA_DK - MLA_NOPE - MLA_ROPE)))
    return w_in_p, wq.reshape(L, R, MLA_HEADS * MLA_DK).astype(_BF16)


def _trunk(x, p, tab):
    for l in range(DEPTH):
        qaT, ka, vaT, qbT, kb, vbT = _project(
            x, p["attn_norm"][l], p["w_in"][l], p["mla_q_norm"][l], p["w_qup"][l],
            p["mla_kv_norm"][l], p["w_kvup"][l], p["gqa_q_norm"][l], p["gqa_k_norm"][l], tab)
        oa = _attention(qaT, ka, vaT, tq=MLA_TQ)
        ob = _attention(qbT, kb, vbT, tq=GQA_TQ)
        x = _mlp(x, oa, ob, p["wo_a"][l], p["wo_b"][l], p["mlp_norm"][l], p["w_up"][l], p["w_dn"][l],
                 p["final_norm"], final=(l == DEPTH - 1))
    return x


def kernel(x_prompt, x_sample, attn_norm, w_in, mla_q_norm, w_mla_q_up, mla_kv_norm, w_mla_kv_up,
           gqa_q_norm, gqa_k_norm, w_out, mlp_norm, w_mlp_up, w_mlp_down, final_norm):
    w_in_p, w_qup = _relayout_weights(w_in, w_mla_q_up)
    row = lambda g: g[:, None, :]
    split = MLA_HEADS * MLA_V
    p = {
        "attn_norm": row(attn_norm), "w_in": w_in_p, "mla_q_norm": row(mla_q_norm), "w_qup": w_qup,
        "mla_kv_norm": row(mla_kv_norm), "w_kvup": w_mla_kv_up.astype(_BF16),
        "gqa_q_norm": row(gqa_q_norm), "gqa_k_norm": row(gqa_k_norm),
        "wo_a": w_out[:, :split, :].astype(_BF16), "wo_b": w_out[:, split:, :].astype(_BF16),
        "mlp_norm": row(mlp_norm), "w_up": w_mlp_up.astype(_BF16), "w_dn": w_mlp_down.astype(_BF16),
        "final_norm": final_norm[None, :],
    }
    tab = _rotary_tables(max(x_prompt.shape[1], x_sample.shape[1]))
    return _trunk(x_prompt, p, tab), _trunk(x_sample, p, tab)
```

```python
import functools
import math

import jax
import jax.numpy as jnp
from jax import lax
from jax.experimental import pallas as pl
from jax.experimental.pallas import tpu as pltpu

D_MODEL = 1024
GRID_W = 64
ROPE_THETA = 10000.0
NORM_EPS = 1e-6
MLA_HEADS = 4
MLA_Q_LORA = 384
MLA_KV_LORA = 256
MLA_NOPE = 128
MLA_ROPE = 64
MLA_V = 128
GQA_HEADS = 4
GQA_KV_HEADS = 2
GQA_HEAD_DIM = 128
D_FF = 4 * D_MODEL
DEPTH = 2

LANES = 128
MXU_DIM = 256
VMEM_LIMIT_BYTES = 56 * 1024 * 1024

PROJ_TM = 512
MLP_TM = 512
KV_CHUNK = 512
ATTN_PROBLEMS = 8
MLA_TQ = 4096
GQA_TQ = 2048
ATTN_UNROLL = 2
FF_CHUNK = 1024

MLA_DK = 2 * LANES
V_EXTRA_ROWS = 16
IN_WIDTH_PADDED = 1792
LOG2E = math.log2(math.e)
NEG_BIG = -1e30

_BF16 = jnp.bfloat16
_F32 = jnp.float32


def _rms(x, g):
    ms = jnp.mean(x * x, axis=-1, keepdims=True)
    return x * lax.rsqrt(ms + NORM_EPS) * g


def _rotary(x, cos, sin_lo, sin_hi, half):
    return x * cos + pltpu.roll(x, LANES - half, 1) * sin_lo + pltpu.roll(x, half, 1) * sin_hi


def _proj_kernel(x_ref, g_attn_ref, w_in_ref, g_q_ref, w_qup_ref, g_kv_ref, w_kvup_ref,
                 g_gq_ref, g_gk_ref, tab_ref,
                 qaT_ref, ka_ref, vaT_ref, qbT_ref, kb_ref, vbT_ref, *, n_chunks):
    tc = KV_CHUNK
    h = _rms(x_ref[...], g_attn_ref[...])
    z = jnp.dot(h.astype(_BF16), w_in_ref[...], preferred_element_type=_F32)
    o_ckv = MLA_Q_LORA
    o_qb = o_ckv + MLA_KV_LORA
    o_kb = o_qb + GQA_HEADS * GQA_HEAD_DIM
    o_vb = o_kb + GQA_KV_HEADS * GQA_HEAD_DIM
    o_kr = o_vb + GQA_KV_HEADS * GQA_HEAD_DIM
    cq = z[:, :o_ckv]
    ckv = z[:, o_ckv:o_qb]

    tab = tab_ref[...]
    g_cos, g_slo, g_shi = tab[:, 0:128], tab[:, 128:256], tab[:, 256:384]
    m_cos, m_slo, m_shi = tab[:, 384:512], tab[:, 512:640], tab[:, 640:768]
    rope_g = functools.partial(_rotary, cos=g_cos, sin_lo=g_slo, sin_hi=g_shi, half=GQA_HEAD_DIM // 4)
    rope_m = functools.partial(_rotary, cos=m_cos, sin_lo=m_slo, sin_hi=m_shi, half=MLA_ROPE // 4)

    scale_a = (MLA_NOPE + MLA_ROPE) ** -0.5 * LOG2E
    scale_b = GQA_HEAD_DIM ** -0.5 * LOG2E

    ones_rows = jnp.where(lax.broadcasted_iota(jnp.int32, (V_EXTRA_ROWS, tc), 0) == 0,
                          1.0, 0.0).astype(_BF16)

    def store_t(ref, head, val):
        dv = val.shape[1]
        for c in range(n_chunks):
            ref[0, head, c, 0:dv, :] = val[c * tc:(c + 1) * tc, :].T.astype(_BF16)
            ref[0, head, c, dv:dv + V_EXTRA_ROWS, :] = ones_rows

    qa = jnp.dot(_rms(cq, g_q_ref[...]).astype(_BF16), w_qup_ref[...],
                 preferred_element_type=_F32)
    kva = jnp.dot(_rms(ckv, g_kv_ref[...]).astype(_BF16), w_kvup_ref[...],
                  preferred_element_type=_F32)
    k_rope = rope_m(z[:, o_kr:o_kr + LANES]).astype(_BF16)
    for hd in range(MLA_HEADS):
        base = hd * MLA_DK
        q_nope = qa[:, base:base + LANES] * scale_a
        q_rope = rope_m(qa[:, base + LANES:base + 2 * LANES]) * scale_a
        qaT_ref[0, hd, 0:LANES, :] = q_nope.T.astype(_BF16)
        qaT_ref[0, hd, LANES:2 * LANES, :] = q_rope.T.astype(_BF16)
        ka_ref[0, hd, :, 0:LANES] = kva[:, base:base + LANES].astype(_BF16)
        ka_ref[0, hd, :, LANES:2 * LANES] = k_rope
        store_t(vaT_ref, hd, kva[:, base + LANES:base + 2 * LANES])

    g_gq = g_gq_ref[...]
    g_gk = g_gk_ref[...]
    for hd in range(GQA_HEADS):
        qh = z[:, o_qb + hd * LANES:o_qb + (hd + 1) * LANES]
        qbT_ref[0, hd] = (rope_g(_rms(qh, g_gq)) * scale_b).T.astype(_BF16)
    for hd in range(GQA_KV_HEADS):
        kh = z[:, o_kb + hd * LANES:o_kb + (hd + 1) * LANES]
        kb_ref[0, hd] = rope_g(_rms(kh, g_gk)).astype(_BF16)
        store_t(vbT_ref, hd, z[:, o_vb + hd * LANES:o_vb + (hd + 1) * LANES])


def _const_spec(shape):
    nd = len(shape)
    return pl.BlockSpec(shape, lambda *_: (0,) * nd)


def _project(x, g_attn, w_in, g_q, w_qup, g_kv, w_kvup, g_gq, g_gk, tab):
    B, S, D = x.shape
    tm = PROJ_TM
    n_chunks = tm // KV_CHUNK
    ns = S // tm
    nc = S // KV_CHUNK
    out_shape = (
        jax.ShapeDtypeStruct((B, MLA_HEADS, MLA_DK, S), _BF16),
        jax.ShapeDtypeStruct((B, MLA_HEADS, S, MLA_DK), _BF16),
        jax.ShapeDtypeStruct((B, MLA_HEADS, nc, MLA_V + V_EXTRA_ROWS, KV_CHUNK), _BF16),
        jax.ShapeDtypeStruct((B, GQA_HEADS, GQA_HEAD_DIM, S), _BF16),
        jax.ShapeDtypeStruct((B, GQA_KV_HEADS, S, GQA_HEAD_DIM), _BF16),
        jax.ShapeDtypeStruct((B, GQA_KV_HEADS, nc, GQA_HEAD_DIM + V_EXTRA_ROWS, KV_CHUNK), _BF16),
    )
    in_specs = [
        pl.BlockSpec((None, tm, D), lambda b, s: (b, s, 0)),
        _const_spec(g_attn.shape), _const_spec(w_in.shape), _const_spec(g_q.shape),
        _const_spec(w_qup.shape), _const_spec(g_kv.shape), _const_spec(w_kvup.shape),
        _const_spec(g_gq.shape), _const_spec(g_gk.shape),
        pl.BlockSpec((tm, tab.shape[1]), lambda b, s: (s, 0)),
    ]
    out_specs = (
        pl.BlockSpec((1, MLA_HEADS, MLA_DK, tm), lambda b, s: (b, 0, 0, s)),
        pl.BlockSpec((1, MLA_HEADS, tm, MLA_DK), lambda b, s: (b, 0, s, 0)),
        pl.BlockSpec((1, MLA_HEADS, n_chunks, MLA_V + V_EXTRA_ROWS, KV_CHUNK), lambda b, s: (b, 0, s, 0, 0)),
        pl.BlockSpec((1, GQA_HEADS, GQA_HEAD_DIM, tm), lambda b, s: (b, 0, 0, s)),
        pl.BlockSpec((1, GQA_KV_HEADS, tm, GQA_HEAD_DIM), lambda b, s: (b, 0, s, 0)),
        pl.BlockSpec((1, GQA_KV_HEADS, n_chunks, GQA_HEAD_DIM + V_EXTRA_ROWS, KV_CHUNK),
                     lambda b, s: (b, 0, s, 0, 0)),
    )
    return pl.pallas_call(
        functools.partial(_proj_kernel, n_chunks=n_chunks),
        grid=(B, ns),
        in_specs=in_specs,
        out_specs=out_specs,
        out_shape=out_shape,
        compiler_params=pltpu.CompilerParams(
            dimension_semantics=("parallel", "parallel"), vmem_limit_bytes=VMEM_LIMIT_BYTES),
        name="proj",
    )(x, g_attn, w_in, g_q, w_qup, g_kv, w_kvup, g_gq, g_gk, tab)


def _attn_kernel(qT_ref, k_ref, vT_ref, o_ref, *scratch, group, n_kv, unroll):
    tc = KV_CHUNK
    sub = tc // 2
    dv = vT_ref.shape[3] - V_EXTRA_ROWS
    n_prob = len(scratch) // 5
    tqp = qT_ref.shape[3] // n_prob
    width = group * tqp
    probs = [scratch[5 * i:5 * i + 5] for i in range(n_prob)]
    q_ts = [jnp.concatenate([qT_ref[0, g, :, i * tqp:(i + 1) * tqp] for g in range(group)], axis=1)
            for i in range(n_prob)]

    def k_rows(j):
        off = pl.multiple_of(j * tc, tc)
        return [k_ref[0, 0, pl.ds(off + r * sub, sub), :] for r in range(2)]

    def store_scores(s_ref, r, d):
        s_ref[r * sub:(r + 1) * sub, :] = d
        return jnp.max(d, axis=0, keepdims=True)

    def exp_block(s_ref, p_ref, r, m_new):
        x = (s_ref[r * sub:(r + 1) * sub, :] - m_new).astype(_BF16)
        p_ref[r * sub:(r + 1) * sub, :] = jnp.exp2(x)

    def flush(pending, c):
        for fn in pending:
            c = fn(c)
        return c

    def step(j, parity, carries, pendings, *, has_pv, has_scores):
        k_sub = k_rows(j + 1) if has_scores else None
        v_blk = vT_ref[0, 0, j - 1] if has_pv else None
        issued = []
        for i, (s_a, s_b, p_a, p_b, _) in enumerate(probs):
            p_prev = (p_a, p_b)[1 - parity]
            dots = ([jnp.dot(kr, q_ts[i], preferred_element_type=_F32) for kr in k_sub]
                    if has_scores else None)
            pv = jnp.dot(v_blk, p_prev[...], preferred_element_type=_F32) if has_pv else None
            issued.append((dots, pv))
        new_carries, new_pendings = [], []
        for i, (s_a, s_b, p_a, p_b, acc_ref) in enumerate(probs):
            m, alpha_prev, c_cur = carries[i]
            dots, pv = issued[i]
            s_cur, p_cur, s_nxt = (s_a, s_b)[parity], (p_a, p_b)[parity], (s_a, s_b)[1 - parity]
            c_cur = flush(pendings[i], c_cur)
            m_new = jnp.maximum(m, c_cur)
            alpha = jnp.exp2(m - m_new)
            exp_block(s_cur, p_cur, 0, m_new)
            c_nxt = store_scores(s_nxt, 0, dots[0]) if has_scores else None
            exp_block(s_cur, p_cur, 1, m_new)
            pending = []
            if has_scores:
                pending.append(lambda c, d=dots[1], ref=s_nxt: jnp.maximum(c, store_scores(ref, 1, d)))
            if has_pv:
                def update_acc(c, pv=pv, a=alpha_prev, acc_ref=acc_ref):
                    acc_ref[...] = a * acc_ref[...] + pv
                    return c
                pending.append(update_acc)
            new_carries.append((m_new, alpha, c_nxt))
            new_pendings.append(pending)
        return new_carries, new_pendings

    def settle(carries, pendings):
        return [c[:2] + (flush(p, c[2]),) for c, p in zip(carries, pendings)]

    carries = []
    k_sub = k_rows(0)
    for i, (s_a, _, _, _, acc_ref) in enumerate(probs):
        acc_ref[...] = jnp.zeros_like(acc_ref)
        d = [jnp.dot(kr, q_ts[i], preferred_element_type=_F32) for kr in k_sub]
        c0 = jnp.maximum(store_scores(s_a, 0, d[0]), store_scores(s_a, 1, d[1]))
        carries.append((jnp.full((1, width), NEG_BIG, _F32), jnp.ones((1, width), _F32), c0))
    no_pending = [[] for _ in probs]
    carries = settle(*step(0, 0, carries, no_pending, has_pv=False, has_scores=True))

    def body(it, carries):
        pendings = no_pending
        for h in range(unroll):
            j = it * unroll + (h + 1)
            carries, pendings = step(j, (h + 1) % 2, carries, pendings, has_pv=True, has_scores=True)
        return settle(carries, pendings)

    carries = lax.fori_loop(0, (n_kv - 2) // unroll, body, carries)
    last = (n_kv - 1) % 2
    carries, pendings = step(n_kv - 1, last, carries, no_pending, has_pv=True, has_scores=False)
    v_blk = vT_ref[0, 0, n_kv - 1]
    for i, (_, _, p_a, p_b, acc_ref) in enumerate(probs):
        flush(pendings[i], None)
        _, alpha, _ = carries[i]
        acc = alpha * acc_ref[...] + jnp.dot(v_blk, (p_a, p_b)[last][...], preferred_element_type=_F32)
        o_t = acc[0:dv, :] / acc[dv:dv + 1, :]
        for g in range(group):
            o_ref[0, i * tqp:(i + 1) * tqp, g * dv:(g + 1) * dv] = (
                o_t[:, g * tqp:(g + 1) * tqp].T.astype(o_ref.dtype))


def _attention(qT, k, vT, *, tq):
    B, Hq, dk, S = qT.shape
    _, Hkv, n_kv, dv_ext, tc = vT.shape
    dv = dv_ext - V_EXTRA_ROWS
    group = Hq // Hkv
    width = group * tq // ATTN_PROBLEMS
    unroll = ATTN_UNROLL
    assert n_kv % 2 == 0 and unroll % 2 == 0 and (n_kv - 2) % unroll == 0
    return pl.pallas_call(
        functools.partial(_attn_kernel, group=group, n_kv=n_kv, unroll=unroll),
        grid=(B, Hkv, S // tq),
        in_specs=[
            pl.BlockSpec((1, group, dk, tq), lambda b, h, q: (b, h, 0, q)),
            pl.BlockSpec((1, 1, S, dk), lambda b, h, q: (b, h, 0, 0), pipeline_mode=pl.Buffered(1)),
            pl.BlockSpec((1, 1, n_kv, dv_ext, tc), lambda b, h, q: (b, h, 0, 0, 0),
                         pipeline_mode=pl.Buffered(1)),
        ],
        out_specs=pl.BlockSpec((1, tq, group * dv), lambda b, h, q: (b, q, h)),
        out_shape=jax.ShapeDtypeStruct((B, S, Hq * dv), _BF16),
        scratch_shapes=[pltpu.VMEM((tc, width), _F32), pltpu.VMEM((tc, width), _F32),
                        pltpu.VMEM((tc, width), _BF16), pltpu.VMEM((tc, width), _BF16),
                        pltpu.VMEM((dv_ext, width), _F32)] * ATTN_PROBLEMS,
        compiler_params=pltpu.CompilerParams(
            dimension_semantics=("parallel", "parallel", "parallel"),
            vmem_limit_bytes=VMEM_LIMIT_BYTES),
        name="attn",
    )(qT, k, vT)


def _mlp_kernel(x_ref, oa_ref, ob_ref, wo_a_ref, wo_b_ref, g_mlp_ref, w_up_ref, w_dn_ref, g_fin_ref,
                y_ref, *, final):
    x1 = (x_ref[...]
          + jnp.dot(oa_ref[...], wo_a_ref[...], preferred_element_type=_F32)
          + jnp.dot(ob_ref[...], wo_b_ref[...], preferred_element_type=_F32))
    hn = _rms(x1, g_mlp_ref[...]).astype(_BF16)
    y_ref[...] = x1
    for c in range(D_FF // FF_CHUNK):
        u = jnp.dot(hn, w_up_ref[:, c * FF_CHUNK:(c + 1) * FF_CHUNK], preferred_element_type=_F32)
        a = jnp.square(jnp.maximum(u, 0.0)).astype(_BF16)
        y_ref[...] += jnp.dot(a, w_dn_ref[c * FF_CHUNK:(c + 1) * FF_CHUNK, :], preferred_element_type=_F32)
    if final:
        y_ref[...] = _rms(y_ref[...], g_fin_ref[...])


def _single_buffered(shape):
    nd = len(shape)
    return pl.BlockSpec(shape, lambda *_: (0,) * nd, pipeline_mode=pl.Buffered(1))


def _mlp(x, oa, ob, wo_a, wo_b, g_mlp, w_up, w_dn, g_fin, *, final):
    B, S, D = x.shape
    tm = MLP_TM
    tok = lambda w: pl.BlockSpec((None, tm, w), lambda b, s: (b, s, 0))
    return pl.pallas_call(
        functools.partial(_mlp_kernel, final=final),
        grid=(B, S // tm),
        in_specs=[
            tok(D), tok(oa.shape[-1]), tok(ob.shape[-1]),
            _single_buffered(wo_a.shape), _single_buffered(wo_b.shape), _const_spec(g_mlp.shape),
            _single_buffered(w_up.shape), _single_buffered(w_dn.shape), _const_spec(g_fin.shape),
        ],
        out_specs=tok(D),
        out_shape=jax.ShapeDtypeStruct((B, S, D), _F32),
        compiler_params=pltpu.CompilerParams(
            dimension_semantics=("parallel", "parallel"), vmem_limit_bytes=VMEM_LIMIT_BYTES),
        name="mlp",
    )(x, oa, ob, wo_a, wo_b, g_mlp, w_up, w_dn, g_fin)


def _rotary_tables(S):
    t = jnp.arange(S)
    row = (t // GRID_W).astype(_F32)[:, None]
    col = (t % GRID_W).astype(_F32)[:, None]
    lane = jnp.arange(LANES)

    def tables(group):
        half = group // 2
        quarter = half // 2
        j = lane % quarter
        freqs = ROPE_THETA ** (-(2.0 * j.astype(_F32)) / half)
        pos = jnp.where((lane % group) < half, row, col)
        ang = pos * freqs[None, :]
        first = (lane % half) < quarter
        cos = jnp.cos(ang)
        sin = jnp.sin(ang)
        return cos, jnp.where(first, -sin, 0.0), jnp.where(first, 0.0, sin)

    return jnp.concatenate(tables(GQA_HEAD_DIM) + tables(MLA_ROPE), axis=1)


def _relayout_weights(w_in, w_mla_q_up):
    o = [0]
    for w in (MLA_Q_LORA, MLA_KV_LORA, MLA_ROPE, GQA_HEADS * GQA_HEAD_DIM,
              GQA_KV_HEADS * GQA_HEAD_DIM, GQA_KV_HEADS * GQA_HEAD_DIM):
        o.append(o[-1] + w)
    cq, ckv, kr, qb, kb, vb = (w_in[..., o[i]:o[i + 1]] for i in range(6))
    pad = jnp.zeros(w_in.shape[:-1] + (LANES - MLA_ROPE,), w_in.dtype)
    w_in_p = jnp.concatenate([cq, ckv, qb, kb, vb, kr, pad], axis=-1).astype(_BF16)
    L, R, _ = w_mla_q_up.shape
    wq = w_mla_q_up.reshape(L, R, MLA_HEADS, MLA_NOPE + MLA_ROPE)
    wq = jnp.pad(wq, ((0, 0), (0, 0), (0, 0), (0, MLA_DK - MLA_NOPE - MLA_ROPE)))
    return w_in_p, wq.reshape(L, R, MLA_HEADS * MLA_DK).astype(_BF16)


def _trunk(x, p, tab):
    for l in range(DEPTH):
        qaT, ka, vaT, qbT, kb, vbT = _project(
            x, p["attn_norm"][l], p["w_in"][l], p["mla_q_norm"][l], p["w_qup"][l],
            p["mla_kv_norm"][l], p["w_kvup"][l], p["gqa_q_norm"][l], p["gqa_k_norm"][l], tab)
        oa = _attention(qaT, ka, vaT, tq=MLA_TQ)
        ob = _attention(qbT, kb, vbT, tq=GQA_TQ)
        x = _mlp(x, oa, ob, p["wo_a"][l], p["wo_b"][l], p["mlp_norm"][l], p["w_up"][l], p["w_dn"][l],
                 p["final_norm"], final=(l == DEPTH - 1))
    return x


def kernel(x_prompt, x_sample, attn_norm, w_in, mla_q_norm, w_mla_q_up, mla_kv_norm, w_mla_kv_up,
           gqa_q_norm, gqa_k_norm, w_out, mlp_norm, w_mlp_up, w_mlp_down, final_norm):
    w_in_p, w_qup = _relayout_weights(w_in, w_mla_q_up)
    row = lambda g: g[:, None, :]
    split = MLA_HEADS * MLA_V
    p = {
        "attn_norm": row(attn_norm), "w_in": w_in_p, "mla_q_norm": row(mla_q_norm), "w_qup": w_qup,
        "mla_kv_norm": row(mla_kv_norm), "w_kvup": w_mla_kv_up.astype(_BF16),
        "gqa_q_norm": row(gqa_q_norm), "gqa_k_norm": row(gqa_k_norm),
        "wo_a": w_out[:, :split, :].astype(_BF16), "wo_b": w_out[:, split:, :].astype(_BF16),
        "mlp_norm": row(mlp_norm), "w_up": w_mlp_up.astype(_BF16), "w_dn": w_mlp_down.astype(_BF16),
        "final_norm": final_norm[None, :],
    }
    tab = _rotary_tables(max(x_prompt.shape[1], x_sample.shape[1]))
    return _trunk(x_prompt, p, tab), _trunk(x_sample, p, tab)
```

```python
import functools
import math

import jax
import jax.numpy as jnp
import numpy as np
from jax import lax
from jax.experimental import pallas as pl
from jax.experimental.pallas import tpu as pltpu

D_MODEL = 1024
GRID_W = 64
ROPE_THETA = 10000.0
NORM_EPS = 1e-6
MLA_HEADS = 4
MLA_Q_LORA = 384
MLA_KV_LORA = 256
MLA_NOPE = 128
MLA_ROPE = 64
MLA_V = 128
GQA_HEADS = 4
GQA_KV_HEADS = 2
GQA_HEAD_DIM = 128
D_FF = 4 * D_MODEL
DEPTH = 2

LANES = 128
MXU_DIM = 256
VMEM_LIMIT_BYTES = 56 * 1024 * 1024

PROJ_TM = 512
MLP_TM = 512
KV_CHUNK = 512
ATTN_PROBLEMS = 8
MLA_TQ = 4096
GQA_TQ = 2048
ATTN_UNROLL = 2
FF_CHUNK = 1024

MLA_DK = 2 * LANES
V_EXTRA_ROWS = 16
LOG2E = math.log2(math.e)
NEG_BIG = -1e30

_BF16 = jnp.bfloat16
_F32 = jnp.float32


def _rms(x, g):
    ms = jnp.mean(x * x, axis=-1, keepdims=True)
    return x * lax.rsqrt(ms + NORM_EPS) * g


def _rotary(x, cos, sin):
    return x * cos + pltpu.roll(x, LANES // 2, 1) * sin


def _proj_kernel(x_ref, g_attn_ref, w_in_ref, g_q_ref, w_qup_ref, g_kv_ref, w_kvup_ref,
                 g_gq_ref, g_gk_ref, tab_ref,
                 qaT_ref, ka_ref, vaT_ref, qbT_ref, kb_ref, vbT_ref, *, n_chunks):
    tc = KV_CHUNK
    h = _rms(x_ref[...], g_attn_ref[...])
    z = jnp.dot(h.astype(_BF16), w_in_ref[...], preferred_element_type=_F32)
    o_ckv = MLA_Q_LORA
    o_qb = o_ckv + MLA_KV_LORA
    o_kb = o_qb + GQA_HEADS * GQA_HEAD_DIM
    o_vb = o_kb + GQA_KV_HEADS * GQA_HEAD_DIM
    o_kr = o_vb + GQA_KV_HEADS * GQA_HEAD_DIM
    cq = z[:, :o_ckv]
    ckv = z[:, o_ckv:o_qb]

    tab = tab_ref[...]
    rope_g = functools.partial(_rotary, cos=tab[:, 0:LANES], sin=tab[:, LANES:2 * LANES])
    rope_m = functools.partial(_rotary, cos=tab[:, 2 * LANES:3 * LANES], sin=tab[:, 3 * LANES:4 * LANES])

    scale_a = (MLA_NOPE + MLA_ROPE) ** -0.5 * LOG2E
    scale_b = GQA_HEAD_DIM ** -0.5 * LOG2E

    ones_rows = jnp.where(lax.broadcasted_iota(jnp.int32, (V_EXTRA_ROWS, tc), 0) == 0,
                          1.0, 0.0).astype(_BF16)

    def store_t(ref, head, val):
        dv = val.shape[1]
        for c in range(n_chunks):
            ref[0, head, c, 0:dv, :] = val[c * tc:(c + 1) * tc, :].T.astype(_BF16)
            ref[0, head, c, dv:dv + V_EXTRA_ROWS, :] = ones_rows

    qa = jnp.dot(_rms(cq, g_q_ref[...]).astype(_BF16), w_qup_ref[...],
                 preferred_element_type=_F32)
    kva = jnp.dot(_rms(ckv, g_kv_ref[...]).astype(_BF16), w_kvup_ref[...],
                  preferred_element_type=_F32)
    k_rope = rope_m(z[:, o_kr:o_kr + LANES]).astype(_BF16)
    for hd in range(MLA_HEADS):
        base = hd * MLA_DK
        q_nope = qa[:, base:base + LANES] * scale_a
        q_rope = rope_m(qa[:, base + LANES:base + 2 * LANES]) * scale_a
        qaT_ref[0, hd, 0:LANES, :] = q_nope.T.astype(_BF16)
        qaT_ref[0, hd, LANES:2 * LANES, :] = q_rope.T.astype(_BF16)
        ka_ref[0, hd, :, 0:LANES] = kva[:, base:base + LANES].astype(_BF16)
        ka_ref[0, hd, :, LANES:2 * LANES] = k_rope
        store_t(vaT_ref, hd, kva[:, base + LANES:base + 2 * LANES])

    g_gq = g_gq_ref[...]
    g_gk = g_gk_ref[...]
    for hd in range(GQA_HEADS):
        qh = z[:, o_qb + hd * LANES:o_qb + (hd + 1) * LANES]
        qbT_ref[0, hd] = (rope_g(_rms(qh, g_gq)) * scale_b).T.astype(_BF16)
    for hd in range(GQA_KV_HEADS):
        kh = z[:, o_kb + hd * LANES:o_kb + (hd + 1) * LANES]
        kb_ref[0, hd] = rope_g(_rms(kh, g_gk)).astype(_BF16)
        store_t(vbT_ref, hd, z[:, o_vb + hd * LANES:o_vb + (hd + 1) * LANES])


def _const_spec(shape):
    nd = len(shape)
    return pl.BlockSpec(shape, lambda *_: (0,) * nd)


def _project(x, g_attn, w_in, g_q, w_qup, g_kv, w_kvup, g_gq, g_gk, tab):
    B, S, D = x.shape
    tm = PROJ_TM
    n_chunks = tm // KV_CHUNK
    ns = S // tm
    nc = S // KV_CHUNK
    out_shape = (
        jax.ShapeDtypeStruct((B, MLA_HEADS, MLA_DK, S), _BF16),
        jax.ShapeDtypeStruct((B, MLA_HEADS, S, MLA_DK), _BF16),
        jax.ShapeDtypeStruct((B, MLA_HEADS, nc, MLA_V + V_EXTRA_ROWS, KV_CHUNK), _BF16),
        jax.ShapeDtypeStruct((B, GQA_HEADS, GQA_HEAD_DIM, S), _BF16),
        jax.ShapeDtypeStruct((B, GQA_KV_HEADS, S, GQA_HEAD_DIM), _BF16),
        jax.ShapeDtypeStruct((B, GQA_KV_HEADS, nc, GQA_HEAD_DIM + V_EXTRA_ROWS, KV_CHUNK), _BF16),
    )
    in_specs = [
        pl.BlockSpec((None, tm, D), lambda b, s: (b, s, 0)),
        _const_spec(g_attn.shape), _const_spec(w_in.shape), _const_spec(g_q.shape),
        _const_spec(w_qup.shape), _const_spec(g_kv.shape), _const_spec(w_kvup.shape),
        _const_spec(g_gq.shape), _const_spec(g_gk.shape),
        pl.BlockSpec((tm, tab.shape[1]), lambda b, s: (s, 0)),
    ]
    out_specs = (
        pl.BlockSpec((1, MLA_HEADS, MLA_DK, tm), lambda b, s: (b, 0, 0, s)),
        pl.BlockSpec((1, MLA_HEADS, tm, MLA_DK), lambda b, s: (b, 0, s, 0)),
        pl.BlockSpec((1, MLA_HEADS, n_chunks, MLA_V + V_EXTRA_ROWS, KV_CHUNK), lambda b, s: (b, 0, s, 0, 0)),
        pl.BlockSpec((1, GQA_HEADS, GQA_HEAD_DIM, tm), lambda b, s: (b, 0, 0, s)),
        pl.BlockSpec((1, GQA_KV_HEADS, tm, GQA_HEAD_DIM), lambda b, s: (b, 0, s, 0)),
        pl.BlockSpec((1, GQA_KV_HEADS, n_chunks, GQA_HEAD_DIM + V_EXTRA_ROWS, KV_CHUNK),
                     lambda b, s: (b, 0, s, 0, 0)),
    )
    return pl.pallas_call(
        functools.partial(_proj_kernel, n_chunks=n_chunks),
        grid=(B, ns),
        in_specs=in_specs,
        out_specs=out_specs,
        out_shape=out_shape,
        compiler_params=pltpu.CompilerParams(
            dimension_semantics=("parallel", "parallel"), vmem_limit_bytes=VMEM_LIMIT_BYTES),
        name="proj",
    )(x, g_attn, w_in, g_q, w_qup, g_kv, w_kvup, g_gq, g_gk, tab)


def _attn_kernel(qT_ref, k_ref, vT_ref, o_ref, *scratch, group, n_kv, unroll):
    tc = KV_CHUNK
    sub = tc // 2
    dv = vT_ref.shape[3] - V_EXTRA_ROWS
    n_prob = len(scratch) // 5
    tqp = qT_ref.shape[3] // n_prob
    width = group * tqp
    probs = [scratch[5 * i:5 * i + 5] for i in range(n_prob)]
    q_ts = [jnp.concatenate([qT_ref[0, g, :, i * tqp:(i + 1) * tqp] for g in range(group)], axis=1)
            for i in range(n_prob)]

    def k_rows(j):
        return k_ref[0, 0, pl.ds(pl.multiple_of(j * tc, tc), tc), :]

    def score_dots(k_blk, q_t):
        d = jnp.dot(k_blk, q_t, preferred_element_type=_F32)
        return [d[r * sub:(r + 1) * sub, :] for r in range(2)]

    def store_scores(s_ref, r, d):
        s_ref[r * sub:(r + 1) * sub, :] = d
        return jnp.max(d, axis=0, keepdims=True)

    def exp_block(s_ref, p_ref, r, m_new):
        x = (s_ref[r * sub:(r + 1) * sub, :] - m_new).astype(_BF16)
        p_ref[r * sub:(r + 1) * sub, :] = jnp.exp2(x)

    def flush(pending, c):
        for fn in pending:
            c = fn(c)
        return c

    def step(j, parity, carries, pendings, *, has_pv, has_scores):
        k_sub = k_rows(j + 1) if has_scores else None
        v_blk = vT_ref[0, 0, j - 1] if has_pv else None
        issued = []
        for i, (s_a, s_b, p_a, p_b, _) in enumerate(probs):
            p_prev = (p_a, p_b)[1 - parity]
            dots = score_dots(k_sub, q_ts[i]) if has_scores else None
            pv = jnp.dot(v_blk, p_prev[...], preferred_element_type=_F32) if has_pv else None
            issued.append((dots, pv))
        new_carries, new_pendings = [], []
        for i, (s_a, s_b, p_a, p_b, acc_ref) in enumerate(probs):
            m, alpha_prev, c_cur = carries[i]
            dots, pv = issued[i]
            s_cur, p_cur, s_nxt = (s_a, s_b)[parity], (p_a, p_b)[parity], (s_a, s_b)[1 - parity]
            c_cur = flush(pendings[i], c_cur)
            m_new = jnp.maximum(m, c_cur)
            alpha = jnp.exp2(m - m_new)
            exp_block(s_cur, p_cur, 0, m_new)
            c_nxt = store_scores(s_nxt, 0, dots[0]) if has_scores else None
            exp_block(s_cur, p_cur, 1, m_new)
            pending = []
            if has_scores:
                pending.append(lambda c, d=dots[1], ref=s_nxt: jnp.maximum(c, store_scores(ref, 1, d)))
            if has_pv:
                def update_acc(c, pv=pv, a=alpha_prev, acc_ref=acc_ref):
                    acc_ref[...] = a * acc_ref[...] + pv
                    return c
                pending.append(update_acc)
            new_carries.append((m_new, alpha, c_nxt))
            new_pendings.append(pending)
        return new_carries, new_pendings

    def settle(carries, pendings):
        return [c[:2] + (flush(p, c[2]),) for c, p in zip(carries, pendings)]

    carries = []
    k_sub = k_rows(0)
    for i, (s_a, _, _, _, acc_ref) in enumerate(probs):
        acc_ref[...] = jnp.zeros_like(acc_ref)
        d = score_dots(k_sub, q_ts[i])
        c0 = jnp.maximum(store_scores(s_a, 0, d[0]), store_scores(s_a, 1, d[1]))
        carries.append((jnp.full((1, width), NEG_BIG, _F32), jnp.ones((1, width), _F32), c0))
    no_pending = [[] for _ in probs]
    carries = settle(*step(0, 0, carries, no_pending, has_pv=False, has_scores=True))

    def body(it, carries):
        pendings = no_pending
        for h in range(unroll):
            j = it * unroll + (h + 1)
            carries, pendings = step(j, (h + 1) % 2, carries, pendings, has_pv=True, has_scores=True)
        return settle(carries, pendings)

    carries = lax.fori_loop(0, (n_kv - 2) // unroll, body, carries)
    last = (n_kv - 1) % 2
    carries, pendings = step(n_kv - 1, last, carries, no_pending, has_pv=True, has_scores=False)
    v_blk = vT_ref[0, 0, n_kv - 1]
    for i, (_, _, p_a, p_b, acc_ref) in enumerate(probs):
        flush(pendings[i], None)
        _, alpha, _ = carries[i]
        acc = alpha * acc_ref[...] + jnp.dot(v_blk, (p_a, p_b)[last][...], preferred_element_type=_F32)
        o_t = acc[0:dv, :] / acc[dv:dv + 1, :]
        for g in range(group):
            o_ref[0, i * tqp:(i + 1) * tqp, g * dv:(g + 1) * dv] = (
                o_t[:, g * tqp:(g + 1) * tqp].T.astype(o_ref.dtype))


def _attention(qT, k, vT, *, tq):
    B, Hq, dk, S = qT.shape
    _, Hkv, n_kv, dv_ext, tc = vT.shape
    dv = dv_ext - V_EXTRA_ROWS
    group = Hq // Hkv
    width = group * tq // ATTN_PROBLEMS
    unroll = ATTN_UNROLL
    assert n_kv % 2 == 0 and unroll % 2 == 0 and (n_kv - 2) % unroll == 0
    return pl.pallas_call(
        functools.partial(_attn_kernel, group=group, n_kv=n_kv, unroll=unroll),
        grid=(B, Hkv, S // tq),
        in_specs=[
            pl.BlockSpec((1, group, dk, tq), lambda b, h, q: (b, h, 0, q)),
            pl.BlockSpec((1, 1, S, dk), lambda b, h, q: (b, h, 0, 0), pipeline_mode=pl.Buffered(1)),
            pl.BlockSpec((1, 1, n_kv, dv_ext, tc), lambda b, h, q: (b, h, 0, 0, 0),
                         pipeline_mode=pl.Buffered(1)),
        ],
        out_specs=pl.BlockSpec((1, tq, group * dv), lambda b, h, q: (b, q, h)),
        out_shape=jax.ShapeDtypeStruct((B, S, Hq * dv), _BF16),
        scratch_shapes=[pltpu.VMEM((tc, width), _F32), pltpu.VMEM((tc, width), _F32),
                        pltpu.VMEM((tc, width), _BF16), pltpu.VMEM((tc, width), _BF16),
                        pltpu.VMEM((dv_ext, width), _F32)] * ATTN_PROBLEMS,
        compiler_params=pltpu.CompilerParams(
            dimension_semantics=("parallel", "parallel", "parallel"),
            vmem_limit_bytes=VMEM_LIMIT_BYTES),
        name="attn",
    )(qT, k, vT)


def _mlp_kernel(x_ref, oa_ref, ob_ref, wo_a_ref, wo_b_ref, g_mlp_ref, w_up_ref, w_dn_ref, g_fin_ref,
                y_ref, *, final):
    x1 = (x_ref[...]
          + jnp.dot(oa_ref[...], wo_a_ref[...], preferred_element_type=_F32)
          + jnp.dot(ob_ref[...], wo_b_ref[...], preferred_element_type=_F32))
    hn = _rms(x1, g_mlp_ref[...]).astype(_BF16)
    y_ref[...] = x1
    for c in range(D_FF // FF_CHUNK):
        u = jnp.dot(hn, w_up_ref[:, c * FF_CHUNK:(c + 1) * FF_CHUNK], preferred_element_type=_F32)
        a = jnp.square(jnp.maximum(u, 0.0)).astype(_BF16)
        y_ref[...] += jnp.dot(a, w_dn_ref[c * FF_CHUNK:(c + 1) * FF_CHUNK, :], preferred_element_type=_F32)
    if final:
        y_ref[...] = _rms(y_ref[...], g_fin_ref[...])


def _single_buffered(shape):
    nd = len(shape)
    return pl.BlockSpec(shape, lambda *_: (0,) * nd, pipeline_mode=pl.Buffered(1))


def _mlp(x, oa, ob, wo_a, wo_b, g_mlp, w_up, w_dn, g_fin, *, final):
    B, S, D = x.shape
    tm = MLP_TM
    tok = lambda w: pl.BlockSpec((None, tm, w), lambda b, s: (b, s, 0))
    return pl.pallas_call(
        functools.partial(_mlp_kernel, final=final),
        grid=(B, S // tm),
        in_specs=[
            tok(D), tok(oa.shape[-1]), tok(ob.shape[-1]),
            _single_buffered(wo_a.shape), _single_buffered(wo_b.shape), _const_spec(g_mlp.shape),
            _single_buffered(w_up.shape), _single_buffered(w_dn.shape), _const_spec(g_fin.shape),
        ],
        out_specs=tok(D),
        out_shape=jax.ShapeDtypeStruct((B, S, D), _F32),
        compiler_params=pltpu.CompilerParams(
            dimension_semantics=("parallel", "parallel"), vmem_limit_bytes=VMEM_LIMIT_BYTES),
        name="mlp",
    )(x, oa, ob, wo_a, wo_b, g_mlp, w_up, w_dn, g_fin)


def _paired_layout(width):
    q = width // 4
    lane = np.arange(LANES)
    side, r = lane // (LANES // 2), lane % (LANES // 2)
    axis, j = r // q, r % q
    src = np.where(axis < 2, axis * 2 * q + side * q + j, -1)
    return src, j, axis == 0, side == 0


def _take_cols(w, src):
    w_ext = jnp.concatenate([w, jnp.zeros(w.shape[:-1] + (1,), w.dtype)], axis=-1)
    return jnp.take(w_ext, np.where(src < 0, w.shape[-1], src), axis=-1)


def _rotary_tables(S):
    t = jnp.arange(S)
    row = (t // GRID_W).astype(_F32)[:, None]
    col = (t % GRID_W).astype(_F32)[:, None]
    out = []
    for width in (GQA_HEAD_DIM, MLA_ROPE):
        _, j, by_row, x1_side = _paired_layout(width)
        freqs = ROPE_THETA ** (-(2.0 * j.astype(np.float32)) / (width // 2))
        ang = jnp.where(by_row, row, col) * freqs[None, :]
        sin = jnp.sin(ang)
        out += [jnp.cos(ang), jnp.where(x1_side, -sin, sin)]
    return jnp.concatenate(out, axis=1)


def _relayout_weights(w_in, w_mla_q_up):
    o = [0]
    for w in (MLA_Q_LORA, MLA_KV_LORA, MLA_ROPE, GQA_HEADS * GQA_HEAD_DIM,
              GQA_KV_HEADS * GQA_HEAD_DIM, GQA_KV_HEADS * GQA_HEAD_DIM):
        o.append(o[-1] + w)
    cq, ckv, kr, qb, kb, vb = (w_in[..., o[i]:o[i + 1]] for i in range(6))
    src_g = _paired_layout(GQA_HEAD_DIM)[0]
    src_m = _paired_layout(MLA_ROPE)[0]
    heads = lambda w, n: [_take_cols(w[..., h * GQA_HEAD_DIM:(h + 1) * GQA_HEAD_DIM], src_g) for h in range(n)]
    w_in_p = jnp.concatenate([cq, ckv] + heads(qb, GQA_HEADS) + heads(kb, GQA_KV_HEADS)
                             + [vb, _take_cols(kr, src_m)], axis=-1).astype(_BF16)
    L, R, _ = w_mla_q_up.shape
    wq = w_mla_q_up.reshape(L, R, MLA_HEADS, MLA_NOPE + MLA_ROPE)
    wq = jnp.concatenate([wq[..., :MLA_NOPE], _take_cols(wq[..., MLA_NOPE:], src_m)], axis=-1)
    return w_in_p, wq.reshape(L, R, MLA_HEADS * MLA_DK).astype(_BF16)


def _trunk(x, p, tab):
    for l in range(DEPTH):
        qaT, ka, vaT, qbT, kb, vbT = _project(
            x, p["attn_norm"][l], p["w_in"][l], p["mla_q_norm"][l], p["w_qup"][l],
            p["mla_kv_norm"][l], p["w_kvup"][l], p["gqa_q_norm"][l], p["gqa_k_norm"][l], tab)
        oa = _attention(qaT, ka, vaT, tq=MLA_TQ)
        ob = _attention(qbT, kb, vbT, tq=GQA_TQ)
        x = _mlp(x, oa, ob, p["wo_a"][l], p["wo_b"][l], p["mlp_norm"][l], p["w_up"][l], p["w_dn"][l],
                 p["final_norm"], final=(l == DEPTH - 1))
    return x


def kernel(x_prompt, x_sample, attn_norm, w_in, mla_q_norm, w_mla_q_up, mla_kv_norm, w_mla_kv_up,
           gqa_q_norm, gqa_k_norm, w_out, mlp_norm, w_mlp_up, w_mlp_down, final_norm):
    w_in_p, w_qup = _relayout_weights(w_in, w_mla_q_up)
    row = lambda g: g[:, None, :]
    split = MLA_HEADS * MLA_V
    p = {
        "attn_norm": row(attn_norm), "w_in": w_in_p, "mla_q_norm": row(mla_q_norm), "w_qup": w_qup,
        "mla_kv_norm": row(mla_kv_norm), "w_kvup": w_mla_kv_up.astype(_BF16),
        "gqa_q_norm": row(_take_cols(gqa_q_norm, _paired_layout(GQA_HEAD_DIM)[0])),
        "gqa_k_norm": row(_take_cols(gqa_k_norm, _paired_layout(GQA_HEAD_DIM)[0])),
        "wo_a": w_out[:, :split, :].astype(_BF16), "wo_b": w_out[:, split:, :].astype(_BF16),
        "mlp_norm": row(mlp_norm), "w_up": w_mlp_up.astype(_BF16), "w_dn": w_mlp_down.astype(_BF16),
        "final_norm": final_norm[None, :],
    }
    tab = _rotary_tables(max(x_prompt.shape[1], x_sample.shape[1]))
    return _trunk(x_prompt, p, tab), _trunk(x_sample, p, tab)
```

```python
import functools
import math

import jax
import jax.numpy as jnp
import numpy as np
from jax import lax
from jax.experimental import pallas as pl
from jax.experimental.pallas import tpu as pltpu

D_MODEL = 1024
GRID_W = 64
ROPE_THETA = 10000.0
NORM_EPS = 1e-6
MLA_HEADS = 4
MLA_Q_LORA = 384
MLA_KV_LORA = 256
MLA_NOPE = 128
MLA_ROPE = 64
MLA_V = 128
GQA_HEADS = 4
GQA_KV_HEADS = 2
GQA_HEAD_DIM = 128
D_FF = 4 * D_MODEL
DEPTH = 2

LANES = 128
MXU_DIM = 256
VMEM_LIMIT_BYTES = 56 * 1024 * 1024

PROJ_TM = 1024
MLP_TM = 512
KV_CHUNK = 512
ATTN_PROBLEMS = 8
MLA_TQ = 4096
GQA_TQ = 2048
ATTN_UNROLL = 2
FF_CHUNK = 1024

MLA_DK = 2 * LANES
V_EXTRA_ROWS = 16
LOG2E = math.log2(math.e)
NEG_BIG = -1e30

_BF16 = jnp.bfloat16
_F32 = jnp.float32


def _rms(x, g):
    ms = jnp.mean(x * x, axis=-1, keepdims=True)
    return x * lax.rsqrt(ms + NORM_EPS) * g


def _rotary(x, cos, sin):
    return x * cos + pltpu.roll(x, LANES // 2, 1) * sin


def _proj_kernel(x_ref, g_attn_ref, w_in_ref, g_q_ref, w_qup_ref, g_kv_ref, w_kvup_ref,
                 g_gq_ref, g_gk_ref, tab_ref,
                 qaT_ref, ka_ref, vaT_ref, qbT_ref, kb_ref, vbT_ref, *, n_chunks):
    tc = KV_CHUNK
    o_ckv = MLA_Q_LORA
    o_qb = o_ckv + MLA_KV_LORA
    o_kb = o_qb + GQA_HEADS * GQA_HEAD_DIM
    o_vb = o_kb + GQA_KV_HEADS * GQA_HEAD_DIM
    o_kr = o_vb + GQA_KV_HEADS * GQA_HEAD_DIM
    scale_a = (MLA_NOPE + MLA_ROPE) ** -0.5 * LOG2E
    scale_b = GQA_HEAD_DIM ** -0.5 * LOG2E
    ones_rows = jnp.where(lax.broadcasted_iota(jnp.int32, (V_EXTRA_ROWS, tc), 0) == 0,
                          1.0, 0.0).astype(_BF16)
    g_gq = g_gq_ref[...]
    g_gk = g_gk_ref[...]

    for c in range(n_chunks):
        rows = slice(c * tc, (c + 1) * tc)
        h = _rms(x_ref[rows, :], g_attn_ref[...])
        z = jnp.dot(h.astype(_BF16), w_in_ref[...], preferred_element_type=_F32)
        tab = tab_ref[rows, :]
        rope_g = functools.partial(_rotary, cos=tab[:, 0:LANES], sin=tab[:, LANES:2 * LANES])
        rope_m = functools.partial(_rotary, cos=tab[:, 2 * LANES:3 * LANES], sin=tab[:, 3 * LANES:4 * LANES])

        def store_t(ref, head, val):
            dv = val.shape[1]
            ref[0, head, c, 0:dv, :] = val.T.astype(_BF16)
            ref[0, head, c, dv:dv + V_EXTRA_ROWS, :] = ones_rows

        qa = jnp.dot(_rms(z[:, :o_ckv], g_q_ref[...]).astype(_BF16), w_qup_ref[...],
                     preferred_element_type=_F32)
        kva = jnp.dot(_rms(z[:, o_ckv:o_qb], g_kv_ref[...]).astype(_BF16), w_kvup_ref[...],
                      preferred_element_type=_F32)
        k_rope = rope_m(z[:, o_kr:o_kr + LANES]).astype(_BF16)
        for hd in range(MLA_HEADS):
            base = hd * MLA_DK
            q_nope = qa[:, base:base + LANES] * scale_a
            q_rope = rope_m(qa[:, base + LANES:base + 2 * LANES]) * scale_a
            qaT_ref[0, hd, 0:LANES, rows] = q_nope.T.astype(_BF16)
            qaT_ref[0, hd, LANES:2 * LANES, rows] = q_rope.T.astype(_BF16)
            ka_ref[0, hd, rows, 0:LANES] = kva[:, base:base + LANES].astype(_BF16)
            ka_ref[0, hd, rows, LANES:2 * LANES] = k_rope
            store_t(vaT_ref, hd, kva[:, base + LANES:base + 2 * LANES])

        for hd in range(GQA_HEADS):
            qh = z[:, o_qb + hd * LANES:o_qb + (hd + 1) * LANES]
            qbT_ref[0, hd, :, rows] = (rope_g(_rms(qh, g_gq)) * scale_b).T.astype(_BF16)
        for hd in range(GQA_KV_HEADS):
            kh = z[:, o_kb + hd * LANES:o_kb + (hd + 1) * LANES]
            kb_ref[0, hd, rows, :] = rope_g(_rms(kh, g_gk)).astype(_BF16)
            store_t(vbT_ref, hd, z[:, o_vb + hd * LANES:o_vb + (hd + 1) * LANES])


def _const_spec(shape):
    nd = len(shape)
    return pl.BlockSpec(shape, lambda *_: (0,) * nd)


def _project(x, g_attn, w_in, g_q, w_qup, g_kv, w_kvup, g_gq, g_gk, tab):
    B, S, D = x.shape
    tm = PROJ_TM
    n_chunks = tm // KV_CHUNK
    ns = S // tm
    nc = S // KV_CHUNK
    out_shape = (
        jax.ShapeDtypeStruct((B, MLA_HEADS, MLA_DK, S), _BF16),
        jax.ShapeDtypeStruct((B, MLA_HEADS, S, MLA_DK), _BF16),
        jax.ShapeDtypeStruct((B, MLA_HEADS, nc, MLA_V + V_EXTRA_ROWS, KV_CHUNK), _BF16),
        jax.ShapeDtypeStruct((B, GQA_HEADS, GQA_HEAD_DIM, S), _BF16),
        jax.ShapeDtypeStruct((B, GQA_KV_HEADS, S, GQA_HEAD_DIM), _BF16),
        jax.ShapeDtypeStruct((B, GQA_KV_HEADS, nc, GQA_HEAD_DIM + V_EXTRA_ROWS, KV_CHUNK), _BF16),
    )
    in_specs = [
        pl.BlockSpec((None, tm, D), lambda b, s: (b, s, 0)),
        _const_spec(g_attn.shape), _const_spec(w_in.shape), _const_spec(g_q.shape),
        _const_spec(w_qup.shape), _const_spec(g_kv.shape), _const_spec(w_kvup.shape),
        _const_spec(g_gq.shape), _const_spec(g_gk.shape),
        pl.BlockSpec((tm, tab.shape[1]), lambda b, s: (s, 0)),
    ]
    out_specs = (
        pl.BlockSpec((1, MLA_HEADS, MLA_DK, tm), lambda b, s: (b, 0, 0, s)),
        pl.BlockSpec((1, MLA_HEADS, tm, MLA_DK), lambda b, s: (b, 0, s, 0)),
        pl.BlockSpec((1, MLA_HEADS, n_chunks, MLA_V + V_EXTRA_ROWS, KV_CHUNK), lambda b, s: (b, 0, s, 0, 0)),
        pl.BlockSpec((1, GQA_HEADS, GQA_HEAD_DIM, tm), lambda b, s: (b, 0, 0, s)),
        pl.BlockSpec((1, GQA_KV_HEADS, tm, GQA_HEAD_DIM), lambda b, s: (b, 0, s, 0)),
        pl.BlockSpec((1, GQA_KV_HEADS, n_chunks, GQA_HEAD_DIM + V_EXTRA_ROWS, KV_CHUNK),
                     lambda b, s: (b, 0, s, 0, 0)),
    )
    return pl.pallas_call(
        functools.partial(_proj_kernel, n_chunks=n_chunks),
        grid=(B, ns),
        in_specs=in_specs,
        out_specs=out_specs,
        out_shape=out_shape,
        compiler_params=pltpu.CompilerParams(
            dimension_semantics=("parallel", "parallel"), vmem_limit_bytes=VMEM_LIMIT_BYTES),
        name="proj",
    )(x, g_attn, w_in, g_q, w_qup, g_kv, w_kvup, g_gq, g_gk, tab)


def _attn_kernel(qT_ref, k_ref, vT_ref, o_ref, *scratch, group, n_kv, unroll):
    tc = KV_CHUNK
    sub = tc // 2
    dv = vT_ref.shape[3] - V_EXTRA_ROWS
    n_prob = len(scratch) // 5
    tqp = qT_ref.shape[3] // n_prob
    width = group * tqp
    probs = [scratch[5 * i:5 * i + 5] for i in range(n_prob)]
    q_ts = [jnp.concatenate([qT_ref[0, g, :, i * tqp:(i + 1) * tqp] for g in range(group)], axis=1)
            for i in range(n_prob)]

    def k_rows(j):
        return k_ref[0, 0, pl.ds(pl.multiple_of(j * tc, tc), tc), :]

    def score_dots(k_blk, q_t):
        d = jnp.dot(k_blk, q_t, preferred_element_type=_F32)
        return [d[r * sub:(r + 1) * sub, :] for r in range(2)]

    def store_scores(s_ref, r, d):
        s_ref[r * sub:(r + 1) * sub, :] = d
        return jnp.max(d, axis=0, keepdims=True)

    def exp_block(s_ref, p_ref, r, m_new):
        x = (s_ref[r * sub:(r + 1) * sub, :] - m_new).astype(_BF16)
        p_ref[r * sub:(r + 1) * sub, :] = jnp.exp2(x)

    def flush(pending, c):
        for fn in pending:
            c = fn(c)
        return c

    def step(j, parity, carries, pendings, *, has_pv, has_scores):
        k_sub = k_rows(j + 1) if has_scores else None
        v_blk = vT_ref[0, 0, j - 1] if has_pv else None
        issued = []
        for i, (s_a, s_b, p_a, p_b, _) in enumerate(probs):
            p_prev = (p_a, p_b)[1 - parity]
            dots = score_dots(k_sub, q_ts[i]) if has_scores else None
            pv = jnp.dot(v_blk, p_prev[...], preferred_element_type=_F32) if has_pv else None
            issued.append((dots, pv))
        new_carries, new_pendings = [], []
        for i, (s_a, s_b, p_a, p_b, acc_ref) in enumerate(probs):
            m, alpha_prev, c_cur = carries[i]
            dots, pv = issued[i]
            s_cur, p_cur, s_nxt = (s_a, s_b)[parity], (p_a, p_b)[parity], (s_a, s_b)[1 - parity]
            c_cur = flush(pendings[i], c_cur)
            m_new = jnp.maximum(m, c_cur)
            alpha = jnp.exp2(m - m_new)
            exp_block(s_cur, p_cur, 0, m_new)
            c_nxt = store_scores(s_nxt, 0, dots[0]) if has_scores else None
            exp_block(s_cur, p_cur, 1, m_new)
            pending = []
            if has_scores:
                pending.append(lambda c, d=dots[1], ref=s_nxt: jnp.maximum(c, store_scores(ref, 1, d)))
            if has_pv:
                def update_acc(c, pv=pv, a=alpha_prev, acc_ref=acc_ref):
                    acc_ref[...] = a * acc_ref[...] + pv
                    return c
                pending.append(update_acc)
            new_carries.append((m_new, alpha, c_nxt))
            new_pendings.append(pending)
        return new_carries, new_pendings

    def settle(carries, pendings):
        return [c[:2] + (flush(p, c[2]),) for c, p in zip(carries, pendings)]

    carries = []
    k_sub = k_rows(0)
    for i, (s_a, _, _, _, acc_ref) in enumerate(probs):
        acc_ref[...] = jnp.zeros_like(acc_ref)
        d = score_dots(k_sub, q_ts[i])
        c0 = jnp.maximum(store_scores(s_a, 0, d[0]), store_scores(s_a, 1, d[1]))
        carries.append((jnp.full((1, width), NEG_BIG, _F32), jnp.ones((1, width), _F32), c0))
    no_pending = [[] for _ in probs]
    carries = settle(*step(0, 0, carries, no_pending, has_pv=False, has_scores=True))

    def body(it, carries):
        pendings = no_pending
        for h in range(unroll):
            j = it * unroll + (h + 1)
            carries, pendings = step(j, (h + 1) % 2, carries, pendings, has_pv=True, has_scores=True)
        return settle(carries, pendings)

    carries = lax.fori_loop(0, (n_kv - 2) // unroll, body, carries)
    last = (n_kv - 1) % 2
    carries, pendings = step(n_kv - 1, last, carries, no_pending, has_pv=True, has_scores=False)
    v_blk = vT_ref[0, 0, n_kv - 1]
    for i, (_, _, p_a, p_b, acc_ref) in enumerate(probs):
        flush(pendings[i], None)
        _, alpha, _ = carries[i]
        acc = alpha * acc_ref[...] + jnp.dot(v_blk, (p_a, p_b)[last][...], preferred_element_type=_F32)
        o_t = acc[0:dv, :] / acc[dv:dv + 1, :]
        for g in range(group):
            o_ref[0, i * tqp:(i + 1) * tqp, g * dv:(g + 1) * dv] = (
                o_t[:, g * tqp:(g + 1) * tqp].T.astype(o_ref.dtype))


def _attention(qT, k, vT, *, tq):
    B, Hq, dk, S = qT.shape
    _, Hkv, n_kv, dv_ext, tc = vT.shape
    dv = dv_ext - V_EXTRA_ROWS
    group = Hq // Hkv
    width = group * tq // ATTN_PROBLEMS
    unroll = ATTN_UNROLL
    assert n_kv % 2 == 0 and unroll % 2 == 0 and (n_kv - 2) % unroll == 0
    return pl.pallas_call(
        functools.partial(_attn_kernel, group=group, n_kv=n_kv, unroll=unroll),
        grid=(B, Hkv, S // tq),
        in_specs=[
            pl.BlockSpec((1, group, dk, tq), lambda b, h, q: (b, h, 0, q)),
            pl.BlockSpec((1, 1, S, dk), lambda b, h, q: (b, h, 0, 0), pipeline_mode=pl.Buffered(1)),
            pl.BlockSpec((1, 1, n_kv, dv_ext, tc), lambda b, h, q: (b, h, 0, 0, 0),
                         pipeline_mode=pl.Buffered(1)),
        ],
        out_specs=pl.BlockSpec((1, tq, group * dv), lambda b, h, q: (b, q, h)),
        out_shape=jax.ShapeDtypeStruct((B, S, Hq * dv), _BF16),
        scratch_shapes=[pltpu.VMEM((tc, width), _F32), pltpu.VMEM((tc, width), _F32),
                        pltpu.VMEM((tc, width), _BF16), pltpu.VMEM((tc, width), _BF16),
                        pltpu.VMEM((dv_ext, width), _F32)] * ATTN_PROBLEMS,
        compiler_params=pltpu.CompilerParams(
            dimension_semantics=("parallel", "parallel", "parallel"),
            vmem_limit_bytes=VMEM_LIMIT_BYTES),
        name="attn",
    )(qT, k, vT)


def _mlp_kernel(x_ref, oa_ref, ob_ref, wo_a_ref, wo_b_ref, g_mlp_ref, w_up_ref, w_dn_ref, g_fin_ref,
                y_ref, *, final):
    x1 = (x_ref[...]
          + jnp.dot(oa_ref[...], wo_a_ref[...], preferred_element_type=_F32)
          + jnp.dot(ob_ref[...], wo_b_ref[...], preferred_element_type=_F32))
    hn = _rms(x1, g_mlp_ref[...]).astype(_BF16)
    y_ref[...] = x1
    for c in range(D_FF // FF_CHUNK):
        u = jnp.dot(hn, w_up_ref[:, c * FF_CHUNK:(c + 1) * FF_CHUNK], preferred_element_type=_F32)
        a = jnp.square(jnp.maximum(u, 0.0)).astype(_BF16)
        y_ref[...] += jnp.dot(a, w_dn_ref[c * FF_CHUNK:(c + 1) * FF_CHUNK, :], preferred_element_type=_F32)
    if final:
        y_ref[...] = _rms(y_ref[...], g_fin_ref[...])


def _single_buffered(shape):
    nd = len(shape)
    return pl.BlockSpec(shape, lambda *_: (0,) * nd, pipeline_mode=pl.Buffered(1))


def _mlp(x, oa, ob, wo_a, wo_b, g_mlp, w_up, w_dn, g_fin, *, final):
    B, S, D = x.shape
    tm = MLP_TM
    tok = lambda w: pl.BlockSpec((None, tm, w), lambda b, s: (b, s, 0))
    return pl.pallas_call(
        functools.partial(_mlp_kernel, final=final),
        grid=(B, S // tm),
        in_specs=[
            tok(D), tok(oa.shape[-1]), tok(ob.shape[-1]),
            _single_buffered(wo_a.shape), _single_buffered(wo_b.shape), _const_spec(g_mlp.shape),
            _single_buffered(w_up.shape), _single_buffered(w_dn.shape), _const_spec(g_fin.shape),
        ],
        out_specs=tok(D),
        out_shape=jax.ShapeDtypeStruct((B, S, D), _F32),
        compiler_params=pltpu.CompilerParams(
            dimension_semantics=("parallel", "parallel"), vmem_limit_bytes=VMEM_LIMIT_BYTES),
        name="mlp",
    )(x, oa, ob, wo_a, wo_b, g_mlp, w_up, w_dn, g_fin)


def _paired_layout(width):
    q = width // 4
    lane = np.arange(LANES)
    side, r = lane // (LANES // 2), lane % (LANES // 2)
    axis, j = r // q, r % q
    src = np.where(axis < 2, axis * 2 * q + side * q + j, -1)
    return src, j, axis == 0, side == 0


def _take_cols(w, src):
    w_ext = jnp.concatenate([w, jnp.zeros(w.shape[:-1] + (1,), w.dtype)], axis=-1)
    return jnp.take(w_ext, np.where(src < 0, w.shape[-1], src), axis=-1)


def _rotary_tables(S):
    t = jnp.arange(S)
    row = (t // GRID_W).astype(_F32)[:, None]
    col = (t % GRID_W).astype(_F32)[:, None]
    out = []
    for width in (GQA_HEAD_DIM, MLA_ROPE):
        _, j, by_row, x1_side = _paired_layout(width)
        freqs = ROPE_THETA ** (-(2.0 * j.astype(np.float32)) / (width // 2))
        ang = jnp.where(by_row, row, col) * freqs[None, :]
        sin = jnp.sin(ang)
        out += [jnp.cos(ang), jnp.where(x1_side, -sin, sin)]
    return jnp.concatenate(out, axis=1)


def _relayout_weights(w_in, w_mla_q_up):
    o = [0]
    for w in (MLA_Q_LORA, MLA_KV_LORA, MLA_ROPE, GQA_HEADS * GQA_HEAD_DIM,
              GQA_KV_HEADS * GQA_HEAD_DIM, GQA_KV_HEADS * GQA_HEAD_DIM):
        o.append(o[-1] + w)
    cq, ckv, kr, qb, kb, vb = (w_in[..., o[i]:o[i + 1]] for i in range(6))
    src_g = _paired_layout(GQA_HEAD_DIM)[0]
    src_m = _paired_layout(MLA_ROPE)[0]
    heads = lambda w, n: [_take_cols(w[..., h * GQA_HEAD_DIM:(h + 1) * GQA_HEAD_DIM], src_g) for h in range(n)]
    w_in_p = jnp.concatenate([cq, ckv] + heads(qb, GQA_HEADS) + heads(kb, GQA_KV_HEADS)
                             + [vb, _take_cols(kr, src_m)], axis=-1).astype(_BF16)
    L, R, _ = w_mla_q_up.shape
    wq = w_mla_q_up.reshape(L, R, MLA_HEADS, MLA_NOPE + MLA_ROPE)
    wq = jnp.concatenate([wq[..., :MLA_NOPE], _take_cols(wq[..., MLA_NOPE:], src_m)], axis=-1)
    return w_in_p, wq.reshape(L, R, MLA_HEADS * MLA_DK).astype(_BF16)


def _trunk(x, p, tab):
    for l in range(DEPTH):
        qaT, ka, vaT, qbT, kb, vbT = _project(
            x, p["attn_norm"][l], p["w_in"][l], p["mla_q_norm"][l], p["w_qup"][l],
            p["mla_kv_norm"][l], p["w_kvup"][l], p["gqa_q_norm"][l], p["gqa_k_norm"][l], tab)
        oa = _attention(qaT, ka, vaT, tq=MLA_TQ)
        ob = _attention(qbT, kb, vbT, tq=GQA_TQ)
        x = _mlp(x, oa, ob, p["wo_a"][l], p["wo_b"][l], p["mlp_norm"][l], p["w_up"][l], p["w_dn"][l],
                 p["final_norm"], final=(l == DEPTH - 1))
    return x


def kernel(x_prompt, x_sample, attn_norm, w_in, mla_q_norm, w_mla_q_up, mla_kv_norm, w_mla_kv_up,
           gqa_q_norm, gqa_k_norm, w_out, mlp_norm, w_mlp_up, w_mlp_down, final_norm):
    w_in_p, w_qup = _relayout_weights(w_in, w_mla_q_up)
    row = lambda g: g[:, None, :]
    split = MLA_HEADS * MLA_V
    p = {
        "attn_norm": row(attn_norm), "w_in": w_in_p, "mla_q_norm": row(mla_q_norm), "w_qup": w_qup,
        "mla_kv_norm": row(mla_kv_norm), "w_kvup": w_mla_kv_up.astype(_BF16),
        "gqa_q_norm": row(_take_cols(gqa_q_norm, _paired_layout(GQA_HEAD_DIM)[0])),
        "gqa_k_norm": row(_take_cols(gqa_k_norm, _paired_layout(GQA_HEAD_DIM)[0])),
        "wo_a": w_out[:, :split, :].astype(_BF16), "wo_b": w_out[:, split:, :].astype(_BF16),
        "mlp_norm": row(mlp_norm), "w_up": w_mlp_up.astype(_BF16), "w_dn": w_mlp_down.astype(_BF16),
        "final_norm": final_norm[None, :],
    }
    tab = _rotary_tables(max(x_prompt.shape[1], x_sample.shape[1]))
    return _trunk(x_prompt, p, tab), _trunk(x_sample, p, tab)
```

```python
import functools
import math

import jax
import jax.numpy as jnp
import numpy as np
from jax import lax
from jax.experimental import pallas as pl
from jax.experimental.pallas import tpu as pltpu

D_MODEL = 1024
GRID_W = 64
ROPE_THETA = 10000.0
NORM_EPS = 1e-6
MLA_HEADS = 4
MLA_Q_LORA = 384
MLA_KV_LORA = 256
MLA_NOPE = 128
MLA_ROPE = 64
MLA_V = 128
GQA_HEADS = 4
GQA_KV_HEADS = 2
GQA_HEAD_DIM = 128
D_FF = 4 * D_MODEL
DEPTH = 2

LANES = 128
MXU_DIM = 256
VMEM_LIMIT_BYTES = 56 * 1024 * 1024

PROJ_TM = 1024
MLP_TM = 512
KV_CHUNK = 512
ATTN_PROBLEMS = 8
MLA_TQ = 4096
GQA_TQ = 2048
FF_CHUNK = 1024

MLA_DK = 2 * LANES
V_EXTRA_ROWS = 16
LOG2E = math.log2(math.e)
NEG_BIG = -1e30

_BF16 = jnp.bfloat16
_F32 = jnp.float32


def _rms(x, g):
    ms = jnp.mean(x * x, axis=-1, keepdims=True)
    return x * lax.rsqrt(ms + NORM_EPS) * g


def _rotary(x, cos, sin):
    return x * cos + pltpu.roll(x, LANES // 2, 1) * sin


def _proj_kernel(x_ref, g_attn_ref, w_in_ref, g_q_ref, w_qup_ref, g_kv_ref, w_kvup_ref,
                 g_gq_ref, g_gk_ref, tab_ref,
                 qaT_ref, ka_ref, vaT_ref, qbT_ref, kb_ref, vbT_ref, *, n_chunks):
    tc = KV_CHUNK
    o_ckv = MLA_Q_LORA
    o_qb = o_ckv + MLA_KV_LORA
    o_kb = o_qb + GQA_HEADS * GQA_HEAD_DIM
    o_vb = o_kb + GQA_KV_HEADS * GQA_HEAD_DIM
    o_kr = o_vb + GQA_KV_HEADS * GQA_HEAD_DIM
    scale_a = (MLA_NOPE + MLA_ROPE) ** -0.5 * LOG2E
    scale_b = GQA_HEAD_DIM ** -0.5 * LOG2E
    ones_rows = jnp.where(lax.broadcasted_iota(jnp.int32, (V_EXTRA_ROWS, tc), 0) == 0,
                          1.0, 0.0).astype(_BF16)
    g_gq = g_gq_ref[...]
    g_gk = g_gk_ref[...]

    for c in range(n_chunks):
        rows = slice(c * tc, (c + 1) * tc)
        h = _rms(x_ref[rows, :], g_attn_ref[...])
        z = jnp.dot(h.astype(_BF16), w_in_ref[...], preferred_element_type=_F32)
        tab = tab_ref[rows, :]
        rope_g = functools.partial(_rotary, cos=tab[:, 0:LANES], sin=tab[:, LANES:2 * LANES])
        rope_m = functools.partial(_rotary, cos=tab[:, 2 * LANES:3 * LANES], sin=tab[:, 3 * LANES:4 * LANES])

        def store_t(ref, head, val):
            dv = val.shape[1]
            ref[0, head, c, 0:dv, :] = val.T.astype(_BF16)
            ref[0, head, c, dv:dv + V_EXTRA_ROWS, :] = ones_rows

        qa = jnp.dot(_rms(z[:, :o_ckv], g_q_ref[...]).astype(_BF16), w_qup_ref[...],
                     preferred_element_type=_F32)
        kva = jnp.dot(_rms(z[:, o_ckv:o_qb], g_kv_ref[...]).astype(_BF16), w_kvup_ref[...],
                      preferred_element_type=_F32)
        k_rope = rope_m(z[:, o_kr:o_kr + LANES]).astype(_BF16)
        for hd in range(MLA_HEADS):
            base = hd * MLA_DK
            q_nope = qa[:, base:base + LANES] * scale_a
            q_rope = rope_m(qa[:, base + LANES:base + 2 * LANES]) * scale_a
            qaT_ref[0, hd, 0:LANES, rows] = q_nope.T.astype(_BF16)
            qaT_ref[0, hd, LANES:2 * LANES, rows] = q_rope.T.astype(_BF16)
            ka_ref[0, hd, rows, 0:LANES] = kva[:, base:base + LANES].astype(_BF16)
            ka_ref[0, hd, rows, LANES:2 * LANES] = k_rope
            store_t(vaT_ref, hd, kva[:, base + LANES:base + 2 * LANES])

        for hd in range(GQA_HEADS):
            qh = z[:, o_qb + hd * LANES:o_qb + (hd + 1) * LANES]
            qbT_ref[0, hd, :, rows] = (rope_g(_rms(qh, g_gq)) * scale_b).T.astype(_BF16)
        for hd in range(GQA_KV_HEADS):
            kh = z[:, o_kb + hd * LANES:o_kb + (hd + 1) * LANES]
            kb_ref[0, hd, rows, :] = rope_g(_rms(kh, g_gk)).astype(_BF16)
            store_t(vbT_ref, hd, z[:, o_vb + hd * LANES:o_vb + (hd + 1) * LANES])


def _const_spec(shape):
    nd = len(shape)
    return pl.BlockSpec(shape, lambda *_: (0,) * nd)


def _project(x, g_attn, w_in, g_q, w_qup, g_kv, w_kvup, g_gq, g_gk, tab):
    B, S, D = x.shape
    tm = PROJ_TM
    n_chunks = tm // KV_CHUNK
    ns = S // tm
    nc = S // KV_CHUNK
    out_shape = (
        jax.ShapeDtypeStruct((B, MLA_HEADS, MLA_DK, S), _BF16),
        jax.ShapeDtypeStruct((B, MLA_HEADS, S, MLA_DK), _BF16),
        jax.ShapeDtypeStruct((B, MLA_HEADS, nc, MLA_V + V_EXTRA_ROWS, KV_CHUNK), _BF16),
        jax.ShapeDtypeStruct((B, GQA_HEADS, GQA_HEAD_DIM, S), _BF16),
        jax.ShapeDtypeStruct((B, GQA_KV_HEADS, S, GQA_HEAD_DIM), _BF16),
        jax.ShapeDtypeStruct((B, GQA_KV_HEADS, nc, GQA_HEAD_DIM + V_EXTRA_ROWS, KV_CHUNK), _BF16),
    )
    in_specs = [
        pl.BlockSpec((None, tm, D), lambda b, s: (b, s, 0)),
        _const_spec(g_attn.shape), _const_spec(w_in.shape), _const_spec(g_q.shape),
        _const_spec(w_qup.shape), _const_spec(g_kv.shape), _const_spec(w_kvup.shape),
        _const_spec(g_gq.shape), _const_spec(g_gk.shape),
        pl.BlockSpec((tm, tab.shape[1]), lambda b, s: (s, 0)),
    ]
    out_specs = (
        pl.BlockSpec((1, MLA_HEADS, MLA_DK, tm), lambda b, s: (b, 0, 0, s)),
        pl.BlockSpec((1, MLA_HEADS, tm, MLA_DK), lambda b, s: (b, 0, s, 0)),
        pl.BlockSpec((1, MLA_HEADS, n_chunks, MLA_V + V_EXTRA_ROWS, KV_CHUNK), lambda b, s: (b, 0, s, 0, 0)),
        pl.BlockSpec((1, GQA_HEADS, GQA_HEAD_DIM, tm), lambda b, s: (b, 0, 0, s)),
        pl.BlockSpec((1, GQA_KV_HEADS, tm, GQA_HEAD_DIM), lambda b, s: (b, 0, s, 0)),
        pl.BlockSpec((1, GQA_KV_HEADS, n_chunks, GQA_HEAD_DIM + V_EXTRA_ROWS, KV_CHUNK),
                     lambda b, s: (b, 0, s, 0, 0)),
    )
    return pl.pallas_call(
        functools.partial(_proj_kernel, n_chunks=n_chunks),
        grid=(B, ns),
        in_specs=in_specs,
        out_specs=out_specs,
        out_shape=out_shape,
        compiler_params=pltpu.CompilerParams(
            dimension_semantics=("parallel", "parallel"), vmem_limit_bytes=VMEM_LIMIT_BYTES),
        name="proj",
    )(x, g_attn, w_in, g_q, w_qup, g_kv, w_kvup, g_gq, g_gk, tab)


def _attn_kernel(qT_ref, k_ref, vT_ref, o_ref, *scratch, group, n_kv):
    tc = KV_CHUNK
    dv = vT_ref.shape[3] - V_EXTRA_ROWS
    n_prob = len(scratch) // 5
    tqp = qT_ref.shape[3] // n_prob
    width = group * tqp
    probs = [scratch[5 * i:5 * i + 5] for i in range(n_prob)]
    q_ts = [jnp.concatenate([qT_ref[0, g, :, i * tqp:(i + 1) * tqp] for g in range(group)], axis=1)
            for i in range(n_prob)]

    HALVES = (slice(0, tc // 2), slice(tc // 2, tc))

    def k_rows(j):
        return k_ref[0, 0, pl.ds(pl.multiple_of(j * tc, tc), tc), :]

    def step(j, parity, carries, *, has_pv, has_scores):
        k_blk = k_rows(j + 1) if has_scores else None
        v_blk = vT_ref[0, 0, j - 1] if has_pv else None
        issued = []
        for q_t, (_, _, p_a, p_b, _) in zip(q_ts, probs):
            s_new = jnp.dot(k_blk, q_t, preferred_element_type=_F32) if has_scores else None
            pv = (jnp.dot(v_blk, (p_a, p_b)[1 - parity][...], preferred_element_type=_F32)
                  if has_pv else None)
            issued.append((s_new, pv))
        out = []
        for (s_new, pv), (m, alpha_prev, c_cur), (s_a, s_b, p_a, p_b, acc_ref) in zip(issued, carries, probs):
            m_new = jnp.maximum(m, c_cur)
            alpha = jnp.exp2(m - m_new)
            for rows in HALVES:
                x = ((s_a, s_b)[parity][rows, :] - m_new).astype(_BF16)
                (p_a, p_b)[parity][rows, :] = jnp.exp2(x)
                if has_scores:
                    (s_a, s_b)[1 - parity][rows, :] = s_new[rows, :]
            c_nxt = jnp.max(s_new, axis=0, keepdims=True) if has_scores else None
            if has_pv:
                acc_ref[...] = alpha_prev * acc_ref[...] + pv
            out.append((m_new, alpha, c_nxt))
        return out

    carries = []
    k_blk = k_rows(0)
    for q_t, (s_a, _, _, _, acc_ref) in zip(q_ts, probs):
        acc_ref[...] = jnp.zeros_like(acc_ref)
        s_new = jnp.dot(k_blk, q_t, preferred_element_type=_F32)
        s_a[...] = s_new
        carries.append((jnp.full((1, width), NEG_BIG, _F32), jnp.ones((1, width), _F32),
                        jnp.max(s_new, axis=0, keepdims=True)))
    carries = step(0, 0, carries, has_pv=False, has_scores=True)

    def body(it, carries):
        for h in range(2):
            carries = step(2 * it + 1 + h, (h + 1) % 2, carries, has_pv=True, has_scores=True)
        return carries

    carries = lax.fori_loop(0, (n_kv - 2) // 2, body, carries)
    last = (n_kv - 1) % 2
    carries = step(n_kv - 1, last, carries, has_pv=True, has_scores=False)
    v_blk = vT_ref[0, 0, n_kv - 1]
    for i, (_, _, p_a, p_b, acc_ref) in enumerate(probs):
        _, alpha, _ = carries[i]
        acc = alpha * acc_ref[...] + jnp.dot(v_blk, (p_a, p_b)[last][...], preferred_element_type=_F32)
        o_t = acc[0:dv, :] / acc[dv:dv + 1, :]
        for g in range(group):
            o_ref[0, i * tqp:(i + 1) * tqp, g * dv:(g + 1) * dv] = (
                o_t[:, g * tqp:(g + 1) * tqp].T.astype(o_ref.dtype))


def _attention(qT, k, vT, *, tq):
    B, Hq, dk, S = qT.shape
    _, Hkv, n_kv, dv_ext, tc = vT.shape
    dv = dv_ext - V_EXTRA_ROWS
    group = Hq // Hkv
    width = group * tq // ATTN_PROBLEMS
    assert n_kv >= 2 and n_kv % 2 == 0
    return pl.pallas_call(
        functools.partial(_attn_kernel, group=group, n_kv=n_kv),
        grid=(B, Hkv, S // tq),
        in_specs=[
            pl.BlockSpec((1, group, dk, tq), lambda b, h, q: (b, h, 0, q)),
            pl.BlockSpec((1, 1, S, dk), lambda b, h, q: (b, h, 0, 0), pipeline_mode=pl.Buffered(1)),
            pl.BlockSpec((1, 1, n_kv, dv_ext, tc), lambda b, h, q: (b, h, 0, 0, 0),
                         pipeline_mode=pl.Buffered(1)),
        ],
        out_specs=pl.BlockSpec((1, tq, group * dv), lambda b, h, q: (b, q, h)),
        out_shape=jax.ShapeDtypeStruct((B, S, Hq * dv), _BF16),
        scratch_shapes=[pltpu.VMEM((tc, width), _F32), pltpu.VMEM((tc, width), _F32),
                        pltpu.VMEM((tc, width), _BF16), pltpu.VMEM((tc, width), _BF16),
                        pltpu.VMEM((dv_ext, width), _F32)] * ATTN_PROBLEMS,
        compiler_params=pltpu.CompilerParams(
            dimension_semantics=("parallel", "parallel", "parallel"),
            vmem_limit_bytes=VMEM_LIMIT_BYTES),
        name="attn",
    )(qT, k, vT)


def _mlp_kernel(x_ref, oa_ref, ob_ref, wo_a_ref, wo_b_ref, g_mlp_ref, w_up_ref, w_dn_ref, g_fin_ref,
                y_ref, *, final):
    x1 = (x_ref[...]
          + jnp.dot(oa_ref[...], wo_a_ref[...], preferred_element_type=_F32)
          + jnp.dot(ob_ref[...], wo_b_ref[...], preferred_element_type=_F32))
    hn = _rms(x1, g_mlp_ref[...]).astype(_BF16)
    y_ref[...] = x1
    for c in range(D_FF // FF_CHUNK):
        u = jnp.dot(hn, w_up_ref[:, c * FF_CHUNK:(c + 1) * FF_CHUNK], preferred_element_type=_F32)
        a = jnp.square(jnp.maximum(u, 0.0)).astype(_BF16)
        y_ref[...] += jnp.dot(a, w_dn_ref[c * FF_CHUNK:(c + 1) * FF_CHUNK, :], preferred_element_type=_F32)
    if final:
        y_ref[...] = _rms(y_ref[...], g_fin_ref[...])


def _single_buffered(shape):
    nd = len(shape)
    return pl.BlockSpec(shape, lambda *_: (0,) * nd, pipeline_mode=pl.Buffered(1))


def _mlp(x, oa, ob, wo_a, wo_b, g_mlp, w_up, w_dn, g_fin, *, final):
    B, S, D = x.shape
    tm = MLP_TM
    tok = lambda w: pl.BlockSpec((None, tm, w), lambda b, s: (b, s, 0))
    return pl.pallas_call(
        functools.partial(_mlp_kernel, final=final),
        grid=(B, S // tm),
        in_specs=[
            tok(D), tok(oa.shape[-1]), tok(ob.shape[-1]),
            _single_buffered(wo_a.shape), _single_buffered(wo_b.shape), _const_spec(g_mlp.shape),
            _single_buffered(w_up.shape), _single_buffered(w_dn.shape), _const_spec(g_fin.shape),
        ],
        out_specs=tok(D),
        out_shape=jax.ShapeDtypeStruct((B, S, D), _F32),
        compiler_params=pltpu.CompilerParams(
            dimension_semantics=("parallel", "parallel"), vmem_limit_bytes=VMEM_LIMIT_BYTES),
        name="mlp",
    )(x, oa, ob, wo_a, wo_b, g_mlp, w_up, w_dn, g_fin)


def _paired_layout(width):
    q = width // 4
    lane = np.arange(LANES)
    side, r = lane // (LANES // 2), lane % (LANES // 2)
    axis, j = r // q, r % q
    src = np.where(axis < 2, axis * 2 * q + side * q + j, -1)
    return src, j, axis == 0, side == 0


def _take_cols(w, src):
    w_ext = jnp.concatenate([w, jnp.zeros(w.shape[:-1] + (1,), w.dtype)], axis=-1)
    return jnp.take(w_ext, np.where(src < 0, w.shape[-1], src), axis=-1)


def _rotary_tables(S):
    t = jnp.arange(S)
    row = (t // GRID_W).astype(_F32)[:, None]
    col = (t % GRID_W).astype(_F32)[:, None]
    out = []
    for width in (GQA_HEAD_DIM, MLA_ROPE):
        _, j, by_row, x1_side = _paired_layout(width)
        freqs = ROPE_THETA ** (-(2.0 * j.astype(np.float32)) / (width // 2))
        ang = jnp.where(by_row, row, col) * freqs[None, :]
        sin = jnp.sin(ang)
        out += [jnp.cos(ang), jnp.where(x1_side, -sin, sin)]
    return jnp.concatenate(out, axis=1)


def _relayout_weights(w_in, w_mla_q_up):
    o = [0]
    for w in (MLA_Q_LORA, MLA_KV_LORA, MLA_ROPE, GQA_HEADS * GQA_HEAD_DIM,
              GQA_KV_HEADS * GQA_HEAD_DIM, GQA_KV_HEADS * GQA_HEAD_DIM):
        o.append(o[-1] + w)
    cq, ckv, kr, qb, kb, vb = (w_in[..., o[i]:o[i + 1]] for i in range(6))
    src_g = _paired_layout(GQA_HEAD_DIM)[0]
    src_m = _paired_layout(MLA_ROPE)[0]
    heads = lambda w, n: [_take_cols(w[..., h * GQA_HEAD_DIM:(h + 1) * GQA_HEAD_DIM], src_g) for h in range(n)]
    w_in_p = jnp.concatenate([cq, ckv] + heads(qb, GQA_HEADS) + heads(kb, GQA_KV_HEADS)
                             + [vb, _take_cols(kr, src_m)], axis=-1).astype(_BF16)
    L, R, _ = w_mla_q_up.shape
    wq = w_mla_q_up.reshape(L, R, MLA_HEADS, MLA_NOPE + MLA_ROPE)
    wq = jnp.concatenate([wq[..., :MLA_NOPE], _take_cols(wq[..., MLA_NOPE:], src_m)], axis=-1)
    return w_in_p, wq.reshape(L, R, MLA_HEADS * MLA_DK).astype(_BF16)


def _trunk(x, p, tab):
    for l in range(DEPTH):
        qaT, ka, vaT, qbT, kb, vbT = _project(
            x, p["attn_norm"][l], p["w_in"][l], p["mla_q_norm"][l], p["w_qup"][l],
            p["mla_kv_norm"][l], p["w_kvup"][l], p["gqa_q_norm"][l], p["gqa_k_norm"][l], tab)
        oa = _attention(qaT, ka, vaT, tq=MLA_TQ)
        ob = _attention(qbT, kb, vbT, tq=GQA_TQ)
        x = _mlp(x, oa, ob, p["wo_a"][l], p["wo_b"][l], p["mlp_norm"][l], p["w_up"][l], p["w_dn"][l],
                 p["final_norm"], final=(l == DEPTH - 1))
    return x


def kernel(x_prompt, x_sample, attn_norm, w_in, mla_q_norm, w_mla_q_up, mla_kv_norm, w_mla_kv_up,
           gqa_q_norm, gqa_k_norm, w_out, mlp_norm, w_mlp_up, w_mlp_down, final_norm):
    w_in_p, w_qup = _relayout_weights(w_in, w_mla_q_up)
    row = lambda g: g[:, None, :]
    split = MLA_HEADS * MLA_V
    p = {
        "attn_norm": row(attn_norm), "w_in": w_in_p, "mla_q_norm": row(mla_q_norm), "w_qup": w_qup,
        "mla_kv_norm": row(mla_kv_norm), "w_kvup": w_mla_kv_up.astype(_BF16),
        "gqa_q_norm": row(_take_cols(gqa_q_norm, _paired_layout(GQA_HEAD_DIM)[0])),
        "gqa_k_norm": row(_take_cols(gqa_k_norm, _paired_layout(GQA_HEAD_DIM)[0])),
        "wo_a": w_out[:, :split, :].astype(_BF16), "wo_b": w_out[:, split:, :].astype(_BF16),
        "mlp_norm": row(mlp_norm), "w_up": w_mlp_up.astype(_BF16), "w_dn": w_mlp_down.astype(_BF16),
        "final_norm": final_norm[None, :],
    }
    tab = _rotary_tables(max(x_prompt.shape[1], x_sample.shape[1]))
    return _trunk(x_prompt, p, tab), _trunk(x_sample, p, tab)
```

```python
import functools
import math

import jax
import jax.numpy as jnp
import numpy as np
from jax import lax
from jax.experimental import pallas as pl
from jax.experimental.pallas import tpu as pltpu

D_MODEL = 1024
GRID_W = 64
ROPE_THETA = 10000.0
NORM_EPS = 1e-6
MLA_HEADS = 4
MLA_Q_LORA = 384
MLA_KV_LORA = 256
MLA_NOPE = 128
MLA_ROPE = 64
MLA_V = 128
GQA_HEADS = 4
GQA_KV_HEADS = 2
GQA_HEAD_DIM = 128
D_FF = 4 * D_MODEL
DEPTH = 2

LANES = 128
MXU_DIM = 256
VMEM_LIMIT_BYTES = 56 * 1024 * 1024

PROJ_TM = 1024
PROJ_CHAIN = 256
MLP_TM = 512
KV_CHUNK = 512
ATTN_PROBLEMS = 8
MLA_TQ = 4096
GQA_TQ = 2048
FF_CHUNK = 1024

MLA_DK = 2 * LANES
V_EXTRA_ROWS = 16
LOG2E = math.log2(math.e)
NEG_BIG = -1e30

_BF16 = jnp.bfloat16
_F32 = jnp.float32


def _rms(x, g):
    ms = jnp.mean(x * x, axis=-1, keepdims=True)
    return x * lax.rsqrt(ms + NORM_EPS) * g


def _rotary(x, cos, sin):
    return x * cos + pltpu.roll(x, LANES // 2, 1) * sin


def _proj_kernel(x_ref, g_attn_ref, w_in_ref, g_q_ref, w_qup_ref, g_kv_ref, w_kvup_ref,
                 g_gq_ref, g_gk_ref, tab_ref,
                 qaT_ref, ka_ref, vaT_ref, qbT_ref, kb_ref, vbT_ref, *, n_chunks):
    tc = KV_CHUNK
    o_ckv = MLA_Q_LORA
    o_qb = o_ckv + MLA_KV_LORA
    o_kb = o_qb + GQA_HEADS * GQA_HEAD_DIM
    o_vb = o_kb + GQA_KV_HEADS * GQA_HEAD_DIM
    o_kr = o_vb + GQA_KV_HEADS * GQA_HEAD_DIM
    g_q = g_q_ref[...] * ((MLA_NOPE + MLA_ROPE) ** -0.5 * LOG2E)
    ch = PROJ_CHAIN
    ones_rows = jnp.where(lax.broadcasted_iota(jnp.int32, (V_EXTRA_ROWS, ch), 0) == 0,
                          1.0, 0.0).astype(_BF16)
    g_gq = g_gq_ref[...] * (GQA_HEAD_DIM ** -0.5 * LOG2E)
    g_gk = g_gk_ref[...]

    for c in range(n_chunks * tc // ch):
        rows = slice(c * ch, (c + 1) * ch)
        chunk, cols = (c * ch) // tc, slice((c * ch) % tc, (c * ch) % tc + ch)
        h = _rms(x_ref[rows, :], g_attn_ref[...])
        z = jnp.dot(h.astype(_BF16), w_in_ref[...], preferred_element_type=_F32)
        tab = tab_ref[rows, :]
        rope_g = functools.partial(_rotary, cos=tab[:, 0:LANES], sin=tab[:, LANES:2 * LANES])
        rope_m = functools.partial(_rotary, cos=tab[:, 2 * LANES:3 * LANES], sin=tab[:, 3 * LANES:4 * LANES])

        def store_t(ref, head, val):
            dv = val.shape[1]
            ref[0, head, chunk, 0:dv, cols] = val.T.astype(_BF16)
            ref[0, head, chunk, dv:dv + V_EXTRA_ROWS, cols] = ones_rows

        qa = jnp.dot(_rms(z[:, :o_ckv], g_q).astype(_BF16), w_qup_ref[...],
                     preferred_element_type=_F32)
        kva = jnp.dot(_rms(z[:, o_ckv:o_qb], g_kv_ref[...]).astype(_BF16), w_kvup_ref[...],
                      preferred_element_type=_F32)
        k_rope = rope_m(z[:, o_kr:o_kr + LANES]).astype(_BF16)
        for hd in range(MLA_HEADS):
            base = hd * MLA_DK
            q_nope = qa[:, base:base + LANES]
            q_rope = rope_m(qa[:, base + LANES:base + 2 * LANES])
            qaT_ref[0, hd, 0:LANES, rows] = q_nope.T.astype(_BF16)
            qaT_ref[0, hd, LANES:2 * LANES, rows] = q_rope.T.astype(_BF16)
            ka_ref[0, hd, rows, 0:LANES] = kva[:, base:base + LANES].astype(_BF16)
            ka_ref[0, hd, rows, LANES:2 * LANES] = k_rope
            store_t(vaT_ref, hd, kva[:, base + LANES:base + 2 * LANES])

        for hd in range(GQA_HEADS):
            qh = z[:, o_qb + hd * LANES:o_qb + (hd + 1) * LANES]
            qbT_ref[0, hd, :, rows] = rope_g(_rms(qh, g_gq)).T.astype(_BF16)
        for hd in range(GQA_KV_HEADS):
            kh = z[:, o_kb + hd * LANES:o_kb + (hd + 1) * LANES]
            kb_ref[0, hd, rows, :] = rope_g(_rms(kh, g_gk)).astype(_BF16)
            store_t(vbT_ref, hd, z[:, o_vb + hd * LANES:o_vb + (hd + 1) * LANES])


def _const_spec(shape):
    nd = len(shape)
    return pl.BlockSpec(shape, lambda *_: (0,) * nd)


def _project(x, g_attn, w_in, g_q, w_qup, g_kv, w_kvup, g_gq, g_gk, tab):
    B, S, D = x.shape
    tm = PROJ_TM
    n_chunks = tm // KV_CHUNK
    ns = S // tm
    nc = S // KV_CHUNK
    out_shape = (
        jax.ShapeDtypeStruct((B, MLA_HEADS, MLA_DK, S), _BF16),
        jax.ShapeDtypeStruct((B, MLA_HEADS, S, MLA_DK), _BF16),
        jax.ShapeDtypeStruct((B, MLA_HEADS, nc, MLA_V + V_EXTRA_ROWS, KV_CHUNK), _BF16),
        jax.ShapeDtypeStruct((B, GQA_HEADS, GQA_HEAD_DIM, S), _BF16),
        jax.ShapeDtypeStruct((B, GQA_KV_HEADS, S, GQA_HEAD_DIM), _BF16),
        jax.ShapeDtypeStruct((B, GQA_KV_HEADS, nc, GQA_HEAD_DIM + V_EXTRA_ROWS, KV_CHUNK), _BF16),
    )
    in_specs = [
        pl.BlockSpec((None, tm, D), lambda b, s: (b, s, 0)),
        _const_spec(g_attn.shape), _const_spec(w_in.shape), _const_spec(g_q.shape),
        _const_spec(w_qup.shape), _const_spec(g_kv.shape), _const_spec(w_kvup.shape),
        _const_spec(g_gq.shape), _const_spec(g_gk.shape),
        pl.BlockSpec((tm, tab.shape[1]), lambda b, s: (s, 0)),
    ]
    out_specs = (
        pl.BlockSpec((1, MLA_HEADS, MLA_DK, tm), lambda b, s: (b, 0, 0, s)),
        pl.BlockSpec((1, MLA_HEADS, tm, MLA_DK), lambda b, s: (b, 0, s, 0)),
        pl.BlockSpec((1, MLA_HEADS, n_chunks, MLA_V + V_EXTRA_ROWS, KV_CHUNK), lambda b, s: (b, 0, s, 0, 0)),
        pl.BlockSpec((1, GQA_HEADS, GQA_HEAD_DIM, tm), lambda b, s: (b, 0, 0, s)),
        pl.BlockSpec((1, GQA_KV_HEADS, tm, GQA_HEAD_DIM), lambda b, s: (b, 0, s, 0)),
        pl.BlockSpec((1, GQA_KV_HEADS, n_chunks, GQA_HEAD_DIM + V_EXTRA_ROWS, KV_CHUNK),
                     lambda b, s: (b, 0, s, 0, 0)),
    )
    return pl.pallas_call(
        functools.partial(_proj_kernel, n_chunks=n_chunks),
        grid=(B, ns),
        in_specs=in_specs,
        out_specs=out_specs,
        out_shape=out_shape,
        compiler_params=pltpu.CompilerParams(
            dimension_semantics=("parallel", "parallel"), vmem_limit_bytes=VMEM_LIMIT_BYTES),
        name="proj",
    )(x, g_attn, w_in, g_q, w_qup, g_kv, w_kvup, g_gq, g_gk, tab)


def _attn_kernel(qT_ref, k_ref, vT_ref, o_ref, *scratch, group, n_kv):
    tc = KV_CHUNK
    dv = vT_ref.shape[3] - V_EXTRA_ROWS
    n_prob = len(scratch) // 5
    tqp = qT_ref.shape[3] // n_prob
    width = group * tqp
    probs = [scratch[5 * i:5 * i + 5] for i in range(n_prob)]
    q_ts = [jnp.concatenate([qT_ref[0, g, :, i * tqp:(i + 1) * tqp] for g in range(group)], axis=1)
            for i in range(n_prob)]

    HALVES = (slice(0, tc // 2), slice(tc // 2, tc))

    def k_rows(j):
        return k_ref[0, 0, pl.ds(pl.multiple_of(j * tc, tc), tc), :]

    def step(j, parity, carries, *, has_pv, has_scores):
        k_blk = k_rows(j + 1) if has_scores else None
        v_blk = vT_ref[0, 0, j - 1] if has_pv else None
        issued = []
        for q_t, (_, _, p_a, p_b, _) in zip(q_ts, probs):
            s_new = jnp.dot(k_blk, q_t, preferred_element_type=_F32) if has_scores else None
            pv = (jnp.dot(v_blk, (p_a, p_b)[1 - parity][...], preferred_element_type=_F32)
                  if has_pv else None)
            issued.append((s_new, pv))
        out = []
        for (s_new, pv), (m, alpha_prev, c_cur), (s_a, s_b, p_a, p_b, acc_ref) in zip(issued, carries, probs):
            m_new = jnp.maximum(m, c_cur)
            alpha = jnp.exp2(m - m_new)
            for rows in HALVES:
                x = ((s_a, s_b)[parity][rows, :] - m_new).astype(_BF16)
                (p_a, p_b)[parity][rows, :] = jnp.exp2(x)
                if has_scores:
                    (s_a, s_b)[1 - parity][rows, :] = s_new[rows, :]
            c_nxt = jnp.max(s_new, axis=0, keepdims=True) if has_scores else None
            if has_pv:
                acc_ref[...] = alpha_prev * acc_ref[...] + pv
            out.append((m_new, alpha, c_nxt))
        return out

    carries = []
    k_blk = k_rows(0)
    for q_t, (s_a, _, _, _, acc_ref) in zip(q_ts, probs):
        acc_ref[...] = jnp.zeros_like(acc_ref)
        s_new = jnp.dot(k_blk, q_t, preferred_element_type=_F32)
        s_a[...] = s_new
        carries.append((jnp.full((1, width), NEG_BIG, _F32), jnp.ones((1, width), _F32),
                        jnp.max(s_new, axis=0, keepdims=True)))
    carries = step(0, 0, carries, has_pv=False, has_scores=True)

    def body(it, carries):
        for h in range(2):
            carries = step(2 * it + 1 + h, (h + 1) % 2, carries, has_pv=True, has_scores=True)
        return carries

    carries = lax.fori_loop(0, (n_kv - 2) // 2, body, carries)
    last = (n_kv - 1) % 2
    carries = step(n_kv - 1, last, carries, has_pv=True, has_scores=False)
    v_blk = vT_ref[0, 0, n_kv - 1]
    for i, (_, _, p_a, p_b, acc_ref) in enumerate(probs):
        _, alpha, _ = carries[i]
        acc = alpha * acc_ref[...] + jnp.dot(v_blk, (p_a, p_b)[last][...], preferred_element_type=_F32)
        o_t = acc[0:dv, :] / acc[dv:dv + 1, :]
        for g in range(group):
            o_ref[0, i * tqp:(i + 1) * tqp, g * dv:(g + 1) * dv] = (
                o_t[:, g * tqp:(g + 1) * tqp].T.astype(o_ref.dtype))


def _attention(qT, k, vT, *, tq):
    B, Hq, dk, S = qT.shape
    _, Hkv, n_kv, dv_ext, tc = vT.shape
    dv = dv_ext - V_EXTRA_ROWS
    group = Hq // Hkv
    width = group * tq // ATTN_PROBLEMS
    assert n_kv >= 2 and n_kv % 2 == 0
    return pl.pallas_call(
        functools.partial(_attn_kernel, group=group, n_kv=n_kv),
        grid=(B, Hkv, S // tq),
        in_specs=[
            pl.BlockSpec((1, group, dk, tq), lambda b, h, q: (b, h, 0, q)),
            pl.BlockSpec((1, 1, S, dk), lambda b, h, q: (b, h, 0, 0), pipeline_mode=pl.Buffered(1)),
            pl.BlockSpec((1, 1, n_kv, dv_ext, tc), lambda b, h, q: (b, h, 0, 0, 0),
                         pipeline_mode=pl.Buffered(1)),
        ],
        out_specs=pl.BlockSpec((1, tq, group * dv), lambda b, h, q: (b, q, h)),
        out_shape=jax.ShapeDtypeStruct((B, S, Hq * dv), _BF16),
        scratch_shapes=[pltpu.VMEM((tc, width), _F32), pltpu.VMEM((tc, width), _F32),
                        pltpu.VMEM((tc, width), _BF16), pltpu.VMEM((tc, width), _BF16),
                        pltpu.VMEM((dv_ext, width), _F32)] * ATTN_PROBLEMS,
        compiler_params=pltpu.CompilerParams(
            dimension_semantics=("parallel", "parallel", "parallel"),
            vmem_limit_bytes=VMEM_LIMIT_BYTES),
        name="attn",
    )(qT, k, vT)


def _mlp_kernel(x_ref, oa_ref, ob_ref, wo_a_ref, wo_b_ref, g_mlp_ref, w_up_ref, w_dn_ref, g_fin_ref,
                y_ref, *, final):
    x1 = (x_ref[...]
          + jnp.dot(oa_ref[...], wo_a_ref[...], preferred_element_type=_F32)
          + jnp.dot(ob_ref[...], wo_b_ref[...], preferred_element_type=_F32))
    hn = _rms(x1, g_mlp_ref[...]).astype(_BF16)
    y_ref[...] = x1
    for c in range(D_FF // FF_CHUNK):
        u = jnp.dot(hn, w_up_ref[:, c * FF_CHUNK:(c + 1) * FF_CHUNK], preferred_element_type=_F32)
        a = jnp.square(jnp.maximum(u, 0.0)).astype(_BF16)
        y_ref[...] += jnp.dot(a, w_dn_ref[c * FF_CHUNK:(c + 1) * FF_CHUNK, :], preferred_element_type=_F32)
    if final:
        y_ref[...] = _rms(y_ref[...], g_fin_ref[...])


def _single_buffered(shape):
    nd = len(shape)
    return pl.BlockSpec(shape, lambda *_: (0,) * nd, pipeline_mode=pl.Buffered(1))


def _mlp(x, oa, ob, wo_a, wo_b, g_mlp, w_up, w_dn, g_fin, *, final):
    B, S, D = x.shape
    tm = MLP_TM
    tok = lambda w: pl.BlockSpec((None, tm, w), lambda b, s: (b, s, 0))
    return pl.pallas_call(
        functools.partial(_mlp_kernel, final=final),
        grid=(B, S // tm),
        in_specs=[
            tok(D), tok(oa.shape[-1]), tok(ob.shape[-1]),
            _single_buffered(wo_a.shape), _single_buffered(wo_b.shape), _const_spec(g_mlp.shape),
            _single_buffered(w_up.shape), _single_buffered(w_dn.shape), _const_spec(g_fin.shape),
        ],
        out_specs=tok(D),
        out_shape=jax.ShapeDtypeStruct((B, S, D), _F32),
        compiler_params=pltpu.CompilerParams(
            dimension_semantics=("parallel", "parallel"), vmem_limit_bytes=VMEM_LIMIT_BYTES),
        name="mlp",
    )(x, oa, ob, wo_a, wo_b, g_mlp, w_up, w_dn, g_fin)


def _paired_layout(width):
    q = width // 4
    lane = np.arange(LANES)
    side, r = lane // (LANES // 2), lane % (LANES // 2)
    axis, j = r // q, r % q
    src = np.where(axis < 2, axis * 2 * q + side * q + j, -1)
    return src, j, axis == 0, side == 0


def _take_cols(w, src):
    w_ext = jnp.concatenate([w, jnp.zeros(w.shape[:-1] + (1,), w.dtype)], axis=-1)
    return jnp.take(w_ext, np.where(src < 0, w.shape[-1], src), axis=-1)


def _rotary_tables(S):
    n_rows = S // GRID_W
    row = jnp.arange(n_rows, dtype=_F32)[:, None]
    col = jnp.arange(GRID_W, dtype=_F32)[:, None]
    out = []
    for width in (GQA_HEAD_DIM, MLA_ROPE):
        _, j, by_row, x1_side = _paired_layout(width)
        freqs = ROPE_THETA ** (-(2.0 * j.astype(np.float32)) / (width // 2))
        for fn, sign in ((jnp.cos, 1.0), (jnp.sin, np.where(x1_side, -1.0, 1.0).astype(np.float32))):
            by_r = jnp.broadcast_to((fn(row * freqs) * sign)[:, None, :], (n_rows, GRID_W, LANES))
            by_c = jnp.broadcast_to((fn(col * freqs) * sign)[None, :, :], (n_rows, GRID_W, LANES))
            out.append(jnp.where(by_row, by_r, by_c).reshape(S, LANES))
    return jnp.concatenate(out, axis=1)


def _relayout_weights(w_in, w_mla_q_up):
    o = [0]
    for w in (MLA_Q_LORA, MLA_KV_LORA, MLA_ROPE, GQA_HEADS * GQA_HEAD_DIM,
              GQA_KV_HEADS * GQA_HEAD_DIM, GQA_KV_HEADS * GQA_HEAD_DIM):
        o.append(o[-1] + w)
    cq, ckv, kr, qb, kb, vb = (w_in[..., o[i]:o[i + 1]] for i in range(6))
    src_g = _paired_layout(GQA_HEAD_DIM)[0]
    src_m = _paired_layout(MLA_ROPE)[0]
    heads = lambda w, n: [_take_cols(w[..., h * GQA_HEAD_DIM:(h + 1) * GQA_HEAD_DIM], src_g) for h in range(n)]
    w_in_p = jnp.concatenate([cq, ckv] + heads(qb, GQA_HEADS) + heads(kb, GQA_KV_HEADS)
                             + [vb, _take_cols(kr, src_m)], axis=-1).astype(_BF16)
    L, R, _ = w_mla_q_up.shape
    wq = w_mla_q_up.reshape(L, R, MLA_HEADS, MLA_NOPE + MLA_ROPE)
    wq = jnp.concatenate([wq[..., :MLA_NOPE], _take_cols(wq[..., MLA_NOPE:], src_m)], axis=-1)
    return w_in_p, wq.reshape(L, R, MLA_HEADS * MLA_DK).astype(_BF16)


def _trunk(x, p, tab):
    for l in range(DEPTH):
        qaT, ka, vaT, qbT, kb, vbT = _project(
            x, p["attn_norm"][l], p["w_in"][l], p["mla_q_norm"][l], p["w_qup"][l],
            p["mla_kv_norm"][l], p["w_kvup"][l], p["gqa_q_norm"][l], p["gqa_k_norm"][l], tab)
        oa = _attention(qaT, ka, vaT, tq=MLA_TQ)
        ob = _attention(qbT, kb, vbT, tq=GQA_TQ)
        x = _mlp(x, oa, ob, p["wo_a"][l], p["wo_b"][l], p["mlp_norm"][l], p["w_up"][l], p["w_dn"][l],
                 p["final_norm"], final=(l == DEPTH - 1))
    return x


def kernel(x_prompt, x_sample, attn_norm, w_in, mla_q_norm, w_mla_q_up, mla_kv_norm, w_mla_kv_up,
           gqa_q_norm, gqa_k_norm, w_out, mlp_norm, w_mlp_up, w_mlp_down, final_norm):
    w_in_p, w_qup = _relayout_weights(w_in, w_mla_q_up)
    row = lambda g: g[:, None, :]
    split = MLA_HEADS * MLA_V
    p = {
        "attn_norm": row(attn_norm), "w_in": w_in_p, "mla_q_norm": row(mla_q_norm), "w_qup": w_qup,
        "mla_kv_norm": row(mla_kv_norm), "w_kvup": w_mla_kv_up.astype(_BF16),
        "gqa_q_norm": row(_take_cols(gqa_q_norm, _paired_layout(GQA_HEAD_DIM)[0])),
        "gqa_k_norm": row(_take_cols(gqa_k_norm, _paired_layout(GQA_HEAD_DIM)[0])),
        "wo_a": w_out[:, :split, :].astype(_BF16), "wo_b": w_out[:, split:, :].astype(_BF16),
        "mlp_norm": row(mlp_norm), "w_up": w_mlp_up.astype(_BF16), "w_dn": w_mlp_down.astype(_BF16),
        "final_norm": final_norm[None, :],
    }
    tab = _rotary_tables(max(x_prompt.shape[1], x_sample.shape[1]))
    return _trunk(x_prompt, p, tab), _trunk(x_sample, p, tab)
```

```python
import functools
import math

import jax
import jax.numpy as jnp
import numpy as np
from jax import lax
from jax.experimental import pallas as pl
from jax.experimental.pallas import tpu as pltpu

D_MODEL = 1024
GRID_W = 64
ROPE_THETA = 10000.0
NORM_EPS = 1e-6
MLA_HEADS = 4
MLA_Q_LORA = 384
MLA_KV_LORA = 256
MLA_NOPE = 128
MLA_ROPE = 64
MLA_V = 128
GQA_HEADS = 4
GQA_KV_HEADS = 2
GQA_HEAD_DIM = 128
D_FF = 4 * D_MODEL
DEPTH = 2

LANES = 128
MXU_DIM = 256
VMEM_LIMIT_BYTES = 56 * 1024 * 1024

PROJ_TM = 1024
PROJ_CHAIN = 256
MLP_TM = 512
KV_CHUNK = 512
ATTN_PROBLEMS = 16
MLA_TQ = 4096
GQA_TQ = 2048
FF_CHUNK = 1024

MLA_DK = 2 * LANES
V_EXTRA_ROWS = 16
LOG2E = math.log2(math.e)
NEG_BIG = -1e30

_BF16 = jnp.bfloat16
_F32 = jnp.float32


def _rms(x, g):
    ms = jnp.mean(x * x, axis=-1, keepdims=True)
    return x * lax.rsqrt(ms + NORM_EPS) * g


def _rotary(x, cos, sin):
    return x * cos + pltpu.roll(x, LANES // 2, 1) * sin


def _proj_kernel(x_ref, g_attn_ref, w_in_ref, g_q_ref, w_qup_ref, g_kv_ref, w_kvup_ref,
                 g_gq_ref, g_gk_ref, tab_ref,
                 qaT_ref, ka_ref, vaT_ref, qbT_ref, kb_ref, vbT_ref, *, n_chunks):
    tc = KV_CHUNK
    o_ckv = MLA_Q_LORA
    o_qb = o_ckv + MLA_KV_LORA
    o_kb = o_qb + GQA_HEADS * GQA_HEAD_DIM
    o_vb = o_kb + GQA_KV_HEADS * GQA_HEAD_DIM
    o_kr = o_vb + GQA_KV_HEADS * GQA_HEAD_DIM
    g_q = g_q_ref[...] * ((MLA_NOPE + MLA_ROPE) ** -0.5 * LOG2E)
    ch = PROJ_CHAIN
    ones_rows = jnp.where(lax.broadcasted_iota(jnp.int32, (V_EXTRA_ROWS, ch), 0) == 0,
                          1.0, 0.0).astype(_BF16)
    g_gq = g_gq_ref[...] * (GQA_HEAD_DIM ** -0.5 * LOG2E)
    g_gk = g_gk_ref[...]

    for c in range(n_chunks * tc // ch):
        rows = slice(c * ch, (c + 1) * ch)
        chunk, cols = (c * ch) // tc, slice((c * ch) % tc, (c * ch) % tc + ch)
        h = _rms(x_ref[rows, :], g_attn_ref[...])
        z = jnp.dot(h.astype(_BF16), w_in_ref[...], preferred_element_type=_F32)
        tab = tab_ref[rows, :]
        rope_g = functools.partial(_rotary, cos=tab[:, 0:LANES], sin=tab[:, LANES:2 * LANES])
        rope_m = functools.partial(_rotary, cos=tab[:, 2 * LANES:3 * LANES], sin=tab[:, 3 * LANES:4 * LANES])

        def store_t(ref, head, val):
            dv = val.shape[1]
            ref[0, head, chunk, 0:dv, cols] = val.T.astype(_BF16)
            ref[0, head, chunk, dv:dv + V_EXTRA_ROWS, cols] = ones_rows

        qa = jnp.dot(_rms(z[:, :o_ckv], g_q).astype(_BF16), w_qup_ref[...],
                     preferred_element_type=_F32)
        kva = jnp.dot(_rms(z[:, o_ckv:o_qb], g_kv_ref[...]).astype(_BF16), w_kvup_ref[...],
                      preferred_element_type=_F32)
        k_rope = rope_m(z[:, o_kr:o_kr + LANES]).astype(_BF16)
        for hd in range(MLA_HEADS):
            base = hd * MLA_DK
            q_nope = qa[:, base:base + LANES]
            q_rope = rope_m(qa[:, base + LANES:base + 2 * LANES])
            qaT_ref[0, hd, 0:LANES, rows] = q_nope.T.astype(_BF16)
            qaT_ref[0, hd, LANES:2 * LANES, rows] = q_rope.T.astype(_BF16)
            ka_ref[0, hd, rows, 0:LANES] = kva[:, base:base + LANES].astype(_BF16)
            ka_ref[0, hd, rows, LANES:2 * LANES] = k_rope
            store_t(vaT_ref, hd, kva[:, base + LANES:base + 2 * LANES])

        for hd in range(GQA_HEADS):
            qh = z[:, o_qb + hd * LANES:o_qb + (hd + 1) * LANES]
            qbT_ref[0, hd, :, rows] = rope_g(_rms(qh, g_gq)).T.astype(_BF16)
        for hd in range(GQA_KV_HEADS):
            kh = z[:, o_kb + hd * LANES:o_kb + (hd + 1) * LANES]
            kb_ref[0, hd, rows, :] = rope_g(_rms(kh, g_gk)).astype(_BF16)
            store_t(vbT_ref, hd, z[:, o_vb + hd * LANES:o_vb + (hd + 1) * LANES])


def _const_spec(shape):
    nd = len(shape)
    return pl.BlockSpec(shape, lambda *_: (0,) * nd)


def _project(x, g_attn, w_in, g_q, w_qup, g_kv, w_kvup, g_gq, g_gk, tab):
    B, S, D = x.shape
    tm = PROJ_TM
    n_chunks = tm // KV_CHUNK
    ns = S // tm
    nc = S // KV_CHUNK
    out_shape = (
        jax.ShapeDtypeStruct((B, MLA_HEADS, MLA_DK, S), _BF16),
        jax.ShapeDtypeStruct((B, MLA_HEADS, S, MLA_DK), _BF16),
        jax.ShapeDtypeStruct((B, MLA_HEADS, nc, MLA_V + V_EXTRA_ROWS, KV_CHUNK), _BF16),
        jax.ShapeDtypeStruct((B, GQA_HEADS, GQA_HEAD_DIM, S), _BF16),
        jax.ShapeDtypeStruct((B, GQA_KV_HEADS, S, GQA_HEAD_DIM), _BF16),
        jax.ShapeDtypeStruct((B, GQA_KV_HEADS, nc, GQA_HEAD_DIM + V_EXTRA_ROWS, KV_CHUNK), _BF16),
    )
    in_specs = [
        pl.BlockSpec((None, tm, D), lambda b, s: (b, s, 0)),
        _const_spec(g_attn.shape), _const_spec(w_in.shape), _const_spec(g_q.shape),
        _const_spec(w_qup.shape), _const_spec(g_kv.shape), _const_spec(w_kvup.shape),
        _const_spec(g_gq.shape), _const_spec(g_gk.shape),
        pl.BlockSpec((tm, tab.shape[1]), lambda b, s: (s, 0)),
    ]
    out_specs = (
        pl.BlockSpec((1, MLA_HEADS, MLA_DK, tm), lambda b, s: (b, 0, 0, s)),
        pl.BlockSpec((1, MLA_HEADS, tm, MLA_DK), lambda b, s: (b, 0, s, 0)),
        pl.BlockSpec((1, MLA_HEADS, n_chunks, MLA_V + V_EXTRA_ROWS, KV_CHUNK), lambda b, s: (b, 0, s, 0, 0)),
        pl.BlockSpec((1, GQA_HEADS, GQA_HEAD_DIM, tm), lambda b, s: (b, 0, 0, s)),
        pl.BlockSpec((1, GQA_KV_HEADS, tm, GQA_HEAD_DIM), lambda b, s: (b, 0, s, 0)),
        pl.BlockSpec((1, GQA_KV_HEADS, n_chunks, GQA_HEAD_DIM + V_EXTRA_ROWS, KV_CHUNK),
                     lambda b, s: (b, 0, s, 0, 0)),
    )
    return pl.pallas_call(
        functools.partial(_proj_kernel, n_chunks=n_chunks),
        grid=(B, ns),
        in_specs=in_specs,
        out_specs=out_specs,
        out_shape=out_shape,
        compiler_params=pltpu.CompilerParams(
            dimension_semantics=("parallel", "parallel"), vmem_limit_bytes=VMEM_LIMIT_BYTES),
        name="proj",
    )(x, g_attn, w_in, g_q, w_qup, g_kv, w_kvup, g_gq, g_gk, tab)


def _attn_kernel(qT_ref, k_ref, vT_ref, o_ref, *scratch, group, n_kv):
    tc = KV_CHUNK
    dv = vT_ref.shape[3] - V_EXTRA_ROWS
    n_prob = len(scratch) // 5
    tqp = qT_ref.shape[3] // n_prob
    width = group * tqp
    probs = [scratch[5 * i:5 * i + 5] for i in range(n_prob)]
    q_ts = [jnp.concatenate([qT_ref[0, g, :, i * tqp:(i + 1) * tqp] for g in range(group)], axis=1)
            for i in range(n_prob)]

    HALVES = (slice(0, tc // 2), slice(tc // 2, tc))

    def k_rows(j):
        return k_ref[0, 0, pl.ds(pl.multiple_of(j * tc, tc), tc), :]

    def step(j, parity, carries, *, has_pv, has_scores):
        k_blk = k_rows(j + 1) if has_scores else None
        v_blk = vT_ref[0, 0, j - 1] if has_pv else None
        issued = []
        for q_t, (_, _, p_a, p_b, _) in zip(q_ts, probs):
            s_new = jnp.dot(k_blk, q_t, preferred_element_type=_F32) if has_scores else None
            pv = (jnp.dot(v_blk, (p_a, p_b)[1 - parity][...], preferred_element_type=_F32)
                  if has_pv else None)
            issued.append((s_new, pv))
        out = []
        for (s_new, pv), (m, alpha_prev, c_cur), (s_a, s_b, p_a, p_b, acc_ref) in zip(issued, carries, probs):
            m_new = jnp.maximum(m, c_cur)
            alpha = jnp.exp2(m - m_new)
            for rows in HALVES:
                x = ((s_a, s_b)[parity][rows, :] - m_new).astype(_BF16)
                (p_a, p_b)[parity][rows, :] = jnp.exp2(x)
                if has_scores:
                    (s_a, s_b)[1 - parity][rows, :] = s_new[rows, :]
            c_nxt = jnp.max(s_new, axis=0, keepdims=True) if has_scores else None
            if has_pv:
                acc_ref[...] = alpha_prev * acc_ref[...] + pv
            out.append((m_new, alpha, c_nxt))
        return out

    carries = []
    k_blk = k_rows(0)
    for q_t, (s_a, _, _, _, acc_ref) in zip(q_ts, probs):
        acc_ref[...] = jnp.zeros_like(acc_ref)
        s_new = jnp.dot(k_blk, q_t, preferred_element_type=_F32)
        s_a[...] = s_new
        carries.append((jnp.full((1, width), NEG_BIG, _F32), jnp.ones((1, width), _F32),
                        jnp.max(s_new, axis=0, keepdims=True)))
    carries = step(0, 0, carries, has_pv=False, has_scores=True)

    def body(it, carries):
        for h in range(2):
            carries = step(2 * it + 1 + h, (h + 1) % 2, carries, has_pv=True, has_scores=True)
        return carries

    carries = lax.fori_loop(0, (n_kv - 2) // 2, body, carries)
    last = (n_kv - 1) % 2
    carries = step(n_kv - 1, last, carries, has_pv=True, has_scores=False)
    v_blk = vT_ref[0, 0, n_kv - 1]
    for i, (_, _, p_a, p_b, acc_ref) in enumerate(probs):
        _, alpha, _ = carries[i]
        acc = alpha * acc_ref[...] + jnp.dot(v_blk, (p_a, p_b)[last][...], preferred_element_type=_F32)
        o_t = acc[0:dv, :] / acc[dv:dv + 1, :]
        for g in range(group):
            o_ref[0, i * tqp:(i + 1) * tqp, g * dv:(g + 1) * dv] = (
                o_t[:, g * tqp:(g + 1) * tqp].T.astype(o_ref.dtype))


def _attention(qT, k, vT, *, tq):
    B, Hq, dk, S = qT.shape
    _, Hkv, n_kv, dv_ext, tc = vT.shape
    dv = dv_ext - V_EXTRA_ROWS
    group = Hq // Hkv
    width = group * tq // ATTN_PROBLEMS
    assert n_kv >= 2 and n_kv % 2 == 0
    return pl.pallas_call(
        functools.partial(_attn_kernel, group=group, n_kv=n_kv),
        grid=(B, Hkv, S // tq),
        in_specs=[
            pl.BlockSpec((1, group, dk, tq), lambda b, h, q: (b, h, 0, q)),
            pl.BlockSpec((1, 1, S, dk), lambda b, h, q: (b, h, 0, 0), pipeline_mode=pl.Buffered(1)),
            pl.BlockSpec((1, 1, n_kv, dv_ext, tc), lambda b, h, q: (b, h, 0, 0, 0),
                         pipeline_mode=pl.Buffered(1)),
        ],
        out_specs=pl.BlockSpec((1, tq, group * dv), lambda b, h, q: (b, q, h)),
        out_shape=jax.ShapeDtypeStruct((B, S, Hq * dv), _BF16),
        scratch_shapes=[pltpu.VMEM((tc, width), _F32), pltpu.VMEM((tc, width), _F32),
                        pltpu.VMEM((tc, width), _BF16), pltpu.VMEM((tc, width), _BF16),
                        pltpu.VMEM((dv_ext, width), _F32)] * ATTN_PROBLEMS,
        compiler_params=pltpu.CompilerParams(
            dimension_semantics=("parallel", "parallel", "parallel"),
            vmem_limit_bytes=VMEM_LIMIT_BYTES),
        name="attn",
    )(qT, k, vT)


def _mlp_kernel(x_ref, oa_ref, ob_ref, wo_a_ref, wo_b_ref, g_mlp_ref, w_up_ref, w_dn_ref, g_fin_ref,
                y_ref, *, final):
    x1 = (x_ref[...]
          + jnp.dot(oa_ref[...], wo_a_ref[...], preferred_element_type=_F32)
          + jnp.dot(ob_ref[...], wo_b_ref[...], preferred_element_type=_F32))
    hn = _rms(x1, g_mlp_ref[...]).astype(_BF16)
    y_ref[...] = x1
    for c in range(D_FF // FF_CHUNK):
        u = jnp.dot(hn, w_up_ref[:, c * FF_CHUNK:(c + 1) * FF_CHUNK], preferred_element_type=_F32)
        a = jnp.square(jnp.maximum(u, 0.0)).astype(_BF16)
        y_ref[...] += jnp.dot(a, w_dn_ref[c * FF_CHUNK:(c + 1) * FF_CHUNK, :], preferred_element_type=_F32)
    if final:
        y_ref[...] = _rms(y_ref[...], g_fin_ref[...])


def _single_buffered(shape):
    nd = len(shape)
    return pl.BlockSpec(shape, lambda *_: (0,) * nd, pipeline_mode=pl.Buffered(1))


def _mlp(x, oa, ob, wo_a, wo_b, g_mlp, w_up, w_dn, g_fin, *, final):
    B, S, D = x.shape
    tm = MLP_TM
    tok = lambda w: pl.BlockSpec((None, tm, w), lambda b, s: (b, s, 0))
    return pl.pallas_call(
        functools.partial(_mlp_kernel, final=final),
        grid=(B, S // tm),
        in_specs=[
            tok(D), tok(oa.shape[-1]), tok(ob.shape[-1]),
            _single_buffered(wo_a.shape), _single_buffered(wo_b.shape), _const_spec(g_mlp.shape),
            _single_buffered(w_up.shape), _single_buffered(w_dn.shape), _const_spec(g_fin.shape),
        ],
        out_specs=tok(D),
        out_shape=jax.ShapeDtypeStruct((B, S, D), _F32),
        compiler_params=pltpu.CompilerParams(
            dimension_semantics=("parallel", "parallel"), vmem_limit_bytes=VMEM_LIMIT_BYTES),
        name="mlp",
    )(x, oa, ob, wo_a, wo_b, g_mlp, w_up, w_dn, g_fin)


def _paired_layout(width):
    q = width // 4
    lane = np.arange(LANES)
    side, r = lane // (LANES // 2), lane % (LANES // 2)
    axis, j = r // q, r % q
    src = np.where(axis < 2, axis * 2 * q + side * q + j, -1)
    return src, j, axis == 0, side == 0


def _take_cols(w, src):
    w_ext = jnp.concatenate([w, jnp.zeros(w.shape[:-1] + (1,), w.dtype)], axis=-1)
    return jnp.take(w_ext, np.where(src < 0, w.shape[-1], src), axis=-1)


def _rotary_tables(S):
    n_rows = S // GRID_W
    row = jnp.arange(n_rows, dtype=_F32)[:, None]
    col = jnp.arange(GRID_W, dtype=_F32)[:, None]
    out = []
    for width in (GQA_HEAD_DIM, MLA_ROPE):
        _, j, by_row, x1_side = _paired_layout(width)
        freqs = ROPE_THETA ** (-(2.0 * j.astype(np.float32)) / (width // 2))
        for fn, sign in ((jnp.cos, 1.0), (jnp.sin, np.where(x1_side, -1.0, 1.0).astype(np.float32))):
            by_r = jnp.broadcast_to((fn(row * freqs) * sign)[:, None, :], (n_rows, GRID_W, LANES))
            by_c = jnp.broadcast_to((fn(col * freqs) * sign)[None, :, :], (n_rows, GRID_W, LANES))
            out.append(jnp.where(by_row, by_r, by_c).reshape(S, LANES))
    return jnp.concatenate(out, axis=1)


def _relayout_weights(w_in, w_mla_q_up):
    o = [0]
    for w in (MLA_Q_LORA, MLA_KV_LORA, MLA_ROPE, GQA_HEADS * GQA_HEAD_DIM,
              GQA_KV_HEADS * GQA_HEAD_DIM, GQA_KV_HEADS * GQA_HEAD_DIM):
        o.append(o[-1] + w)
    cq, ckv, kr, qb, kb, vb = (w_in[..., o[i]:o[i + 1]] for i in range(6))
    src_g = _paired_layout(GQA_HEAD_DIM)[0]
    src_m = _paired_layout(MLA_ROPE)[0]
    heads = lambda w, n: [_take_cols(w[..., h * GQA_HEAD_DIM:(h + 1) * GQA_HEAD_DIM], src_g) for h in range(n)]
    w_in_p = jnp.concatenate([cq, ckv] + heads(qb, GQA_HEADS) + heads(kb, GQA_KV_HEADS)
                             + [vb, _take_cols(kr, src_m)], axis=-1).astype(_BF16)
    L, R, _ = w_mla_q_up.shape
    wq = w_mla_q_up.reshape(L, R, MLA_HEADS, MLA_NOPE + MLA_ROPE)
    wq = jnp.concatenate([wq[..., :MLA_NOPE], _take_cols(wq[..., MLA_NOPE:], src_m)], axis=-1)
    return w_in_p, wq.reshape(L, R, MLA_HEADS * MLA_DK).astype(_BF16)


def _trunk(x, p, tab):
    for l in range(DEPTH):
        qaT, ka, vaT, qbT, kb, vbT = _project(
            x, p["attn_norm"][l], p["w_in"][l], p["mla_q_norm"][l], p["w_qup"][l],
            p["mla_kv_norm"][l], p["w_kvup"][l], p["gqa_q_norm"][l], p["gqa_k_norm"][l], tab)
        oa = _attention(qaT, ka, vaT, tq=MLA_TQ)
        ob = _attention(qbT, kb, vbT, tq=GQA_TQ)
        x = _mlp(x, oa, ob, p["wo_a"][l], p["wo_b"][l], p["mlp_norm"][l], p["w_up"][l], p["w_dn"][l],
                 p["final_norm"], final=(l == DEPTH - 1))
    return x


def kernel(x_prompt, x_sample, attn_norm, w_in, mla_q_norm, w_mla_q_up, mla_kv_norm, w_mla_kv_up,
           gqa_q_norm, gqa_k_norm, w_out, mlp_norm, w_mlp_up, w_mlp_down, final_norm):
    w_in_p, w_qup = _relayout_weights(w_in, w_mla_q_up)
    row = lambda g: g[:, None, :]
    split = MLA_HEADS * MLA_V
    p = {
        "attn_norm": row(attn_norm), "w_in": w_in_p, "mla_q_norm": row(mla_q_norm), "w_qup": w_qup,
        "mla_kv_norm": row(mla_kv_norm), "w_kvup": w_mla_kv_up.astype(_BF16),
        "gqa_q_norm": row(_take_cols(gqa_q_norm, _paired_layout(GQA_HEAD_DIM)[0])),
        "gqa_k_norm": row(_take_cols(gqa_k_norm, _paired_layout(GQA_HEAD_DIM)[0])),
        "wo_a": w_out[:, :split, :].astype(_BF16), "wo_b": w_out[:, split:, :].astype(_BF16),
        "mlp_norm": row(mlp_norm), "w_up": w_mlp_up.astype(_BF16), "w_dn": w_mlp_down.astype(_BF16),
        "final_norm": final_norm[None, :],
    }
    tab = _rotary_tables(max(x_prompt.shape[1], x_sample.shape[1]))
    return _trunk(x_prompt, p, tab), _trunk(x_sample, p, tab)
```

```python
import functools
import math

import jax
import jax.numpy as jnp
import numpy as np
from jax import lax
from jax.experimental import pallas as pl
from jax.experimental.pallas import tpu as pltpu

D_MODEL = 1024
GRID_W = 64
ROPE_THETA = 10000.0
NORM_EPS = 1e-6
MLA_HEADS = 4
MLA_Q_LORA = 384
MLA_KV_LORA = 256
MLA_NOPE = 128
MLA_ROPE = 64
MLA_V = 128
GQA_HEADS = 4
GQA_KV_HEADS = 2
GQA_HEAD_DIM = 128
D_FF = 4 * D_MODEL
DEPTH = 2

LANES = 128
MXU_DIM = 256
VMEM_LIMIT_BYTES = 56 * 1024 * 1024

PROJ_TM = 1024
PROJ_CHAIN = 256
MLP_TM = 512
KV_CHUNK = 512
ATTN_PROBLEMS = 16
MLA_TQ = 4096
GQA_TQ = 2048
FF_CHUNK = 1024

MLA_DK = 2 * LANES
V_EXTRA_ROWS = 16
LOG2E = math.log2(math.e)
NEG_BIG = -1e30

_BF16 = jnp.bfloat16
_F32 = jnp.float32


def _rms(x, g):
    ms = jnp.mean(x * x, axis=-1, keepdims=True)
    return x * lax.rsqrt(ms + NORM_EPS) * g


def _rotary(x, cos, sin):
    return x * cos + pltpu.roll(x, LANES // 2, 1) * sin


def _proj_kernel(x_ref, g_attn_ref, w_in_ref, g_q_ref, w_qup_ref, g_kv_ref, w_kvup_ref,
                 g_gq_ref, g_gk_ref, tab_ref,
                 qaT_ref, ka_ref, vaT_ref, qbT_ref, kb_ref, vbT_ref, *, n_chunks):
    tc = KV_CHUNK
    o_ckv = MLA_Q_LORA
    o_qb = o_ckv + MLA_KV_LORA
    o_kb = o_qb + GQA_HEADS * GQA_HEAD_DIM
    o_vb = o_kb + GQA_KV_HEADS * GQA_HEAD_DIM
    o_kr = o_vb + GQA_KV_HEADS * GQA_HEAD_DIM
    g_q = g_q_ref[...] * ((MLA_NOPE + MLA_ROPE) ** -0.5 * LOG2E)
    ch = PROJ_CHAIN
    ones_rows = jnp.where(lax.broadcasted_iota(jnp.int32, (V_EXTRA_ROWS, ch), 0) == 0,
                          1.0, 0.0).astype(_BF16)
    g_gq = g_gq_ref[...] * (GQA_HEAD_DIM ** -0.5 * LOG2E)
    g_gk = g_gk_ref[...]

    for c in range(n_chunks * tc // ch):
        rows = slice(c * ch, (c + 1) * ch)
        chunk, cols = (c * ch) // tc, slice((c * ch) % tc, (c * ch) % tc + ch)
        h = _rms(x_ref[rows, :], g_attn_ref[...])
        z = jnp.dot(h.astype(_BF16), w_in_ref[...], preferred_element_type=_F32)
        tab = tab_ref[rows, :]
        rope_g = functools.partial(_rotary, cos=tab[:, 0:LANES], sin=tab[:, LANES:2 * LANES])
        rope_m = functools.partial(_rotary, cos=tab[:, 2 * LANES:3 * LANES], sin=tab[:, 3 * LANES:4 * LANES])

        def store_t(ref, head, val):
            dv = val.shape[1]
            ref[0, head, chunk, 0:dv, cols] = val.T.astype(_BF16)
            ref[0, head, chunk, dv:dv + V_EXTRA_ROWS, cols] = ones_rows

        qa = jnp.dot(_rms(z[:, :o_ckv], g_q).astype(_BF16), w_qup_ref[...],
                     preferred_element_type=_F32)
        kva = jnp.dot(_rms(z[:, o_ckv:o_qb], g_kv_ref[...]).astype(_BF16), w_kvup_ref[...],
                      preferred_element_type=_F32)
        k_rope = rope_m(z[:, o_kr:o_kr + LANES]).astype(_BF16)
        for hd in range(MLA_HEADS):
            base = hd * MLA_DK
            q_nope = qa[:, base:base + LANES]
            q_rope = rope_m(qa[:, base + LANES:base + 2 * LANES])
            qaT_ref[0, hd, 0:LANES, rows] = q_nope.T.astype(_BF16)
            qaT_ref[0, hd, LANES:2 * LANES, rows] = q_rope.T.astype(_BF16)
            ka_ref[0, hd, rows, 0:LANES] = kva[:, base:base + LANES].astype(_BF16)
            ka_ref[0, hd, rows, LANES:2 * LANES] = k_rope
            store_t(vaT_ref, hd, kva[:, base + LANES:base + 2 * LANES])

        for hd in range(GQA_HEADS):
            qh = z[:, o_qb + hd * LANES:o_qb + (hd + 1) * LANES]
            qbT_ref[0, hd, :, rows] = rope_g(_rms(qh, g_gq)).T.astype(_BF16)
        for hd in range(GQA_KV_HEADS):
            kh = z[:, o_kb + hd * LANES:o_kb + (hd + 1) * LANES]
            kb_ref[0, hd, rows, :] = rope_g(_rms(kh, g_gk)).astype(_BF16)
            store_t(vbT_ref, hd, z[:, o_vb + hd * LANES:o_vb + (hd + 1) * LANES])


def _const_spec(shape):
    nd = len(shape)
    return pl.BlockSpec(shape, lambda *_: (0,) * nd)


def _project(x, g_attn, w_in, g_q, w_qup, g_kv, w_kvup, g_gq, g_gk, tab):
    B, S, D = x.shape
    tm = PROJ_TM
    n_chunks = tm // KV_CHUNK
    ns = S // tm
    nc = S // KV_CHUNK
    out_shape = (
        jax.ShapeDtypeStruct((B, MLA_HEADS, MLA_DK, S), _BF16),
        jax.ShapeDtypeStruct((B, MLA_HEADS, S, MLA_DK), _BF16),
        jax.ShapeDtypeStruct((B, MLA_HEADS, nc, MLA_V + V_EXTRA_ROWS, KV_CHUNK), _BF16),
        jax.ShapeDtypeStruct((B, GQA_HEADS, GQA_HEAD_DIM, S), _BF16),
        jax.ShapeDtypeStruct((B, GQA_KV_HEADS, S, GQA_HEAD_DIM), _BF16),
        jax.ShapeDtypeStruct((B, GQA_KV_HEADS, nc, GQA_HEAD_DIM + V_EXTRA_ROWS, KV_CHUNK), _BF16),
    )
    in_specs = [
        pl.BlockSpec((None, tm, D), lambda b, s: (b, s, 0)),
        _const_spec(g_attn.shape), _const_spec(w_in.shape), _const_spec(g_q.shape),
        _const_spec(w_qup.shape), _const_spec(g_kv.shape), _const_spec(w_kvup.shape),
        _const_spec(g_gq.shape), _const_spec(g_gk.shape),
        pl.BlockSpec((tm, tab.shape[1]), lambda b, s: (s, 0)),
    ]
    out_specs = (
        pl.BlockSpec((1, MLA_HEADS, MLA_DK, tm), lambda b, s: (b, 0, 0, s)),
        pl.BlockSpec((1, MLA_HEADS, tm, MLA_DK), lambda b, s: (b, 0, s, 0)),
        pl.BlockSpec((1, MLA_HEADS, n_chunks, MLA_V + V_EXTRA_ROWS, KV_CHUNK), lambda b, s: (b, 0, s, 0, 0)),
        pl.BlockSpec((1, GQA_HEADS, GQA_HEAD_DIM, tm), lambda b, s: (b, 0, 0, s)),
        pl.BlockSpec((1, GQA_KV_HEADS, tm, GQA_HEAD_DIM), lambda b, s: (b, 0, s, 0)),
        pl.BlockSpec((1, GQA_KV_HEADS, n_chunks, GQA_HEAD_DIM + V_EXTRA_ROWS, KV_CHUNK),
                     lambda b, s: (b, 0, s, 0, 0)),
    )
    return pl.pallas_call(
        functools.partial(_proj_kernel, n_chunks=n_chunks),
        grid=(B, ns),
        in_specs=in_specs,
        out_specs=out_specs,
        out_shape=out_shape,
        compiler_params=pltpu.CompilerParams(
            dimension_semantics=("parallel", "parallel"), vmem_limit_bytes=VMEM_LIMIT_BYTES),
        name="proj",
    )(x, g_attn, w_in, g_q, w_qup, g_kv, w_kvup, g_gq, g_gk, tab)


def _attn_kernel(qT_ref, k_ref, vT_ref, o_ref, *scratch, group, n_kv):
    tc = KV_CHUNK
    dv = vT_ref.shape[3] - V_EXTRA_ROWS
    n_prob = len(scratch) // 5
    tqp = qT_ref.shape[3] // n_prob
    width = group * tqp
    probs = [scratch[5 * i:5 * i + 5] for i in range(n_prob)]
    q_ts = [jnp.concatenate([qT_ref[0, g, :, i * tqp:(i + 1) * tqp] for g in range(group)], axis=1)
            for i in range(n_prob)]

    HALVES = (slice(0, tc // 2), slice(tc // 2, tc))

    def k_rows(j):
        return k_ref[0, 0, pl.ds(pl.multiple_of(j * tc, tc), tc), :]

    def step(j, parity, carries, *, has_pv, has_scores):
        k_blk = k_rows(j + 1) if has_scores else None
        v_blk = vT_ref[0, 0, j - 1] if has_pv else None
        issued = []
        for q_t, (_, _, p_a, p_b, _) in zip(q_ts, probs):
            s_new = jnp.dot(k_blk, q_t, preferred_element_type=_F32) if has_scores else None
            pv = (jnp.dot(v_blk, (p_a, p_b)[1 - parity][...], preferred_element_type=_F32)
                  if has_pv else None)
            issued.append((s_new, pv))
        out = []
        for (s_new, pv), (m, alpha_prev, c_cur), (s_a, s_b, p_a, p_b, acc_ref) in zip(issued, carries, probs):
            m_new = jnp.maximum(m, c_cur)
            alpha = jnp.exp2(m - m_new)
            for rows in HALVES:
                x = ((s_a, s_b)[parity][rows, :] - m_new).astype(_BF16)
                (p_a, p_b)[parity][rows, :] = jnp.exp2(x)
                if has_scores:
                    (s_a, s_b)[1 - parity][rows, :] = s_new[rows, :]
            c_nxt = jnp.max(s_new, axis=0, keepdims=True) if has_scores else None
            if has_pv:
                acc_ref[...] = alpha_prev * acc_ref[...] + pv
            out.append((m_new, alpha, c_nxt))
        return out

    carries = []
    k_blk = k_rows(0)
    for q_t, (s_a, _, _, _, acc_ref) in zip(q_ts, probs):
        acc_ref[...] = jnp.zeros_like(acc_ref)
        s_new = jnp.dot(k_blk, q_t, preferred_element_type=_F32)
        s_a[...] = s_new
        carries.append((jnp.full((1, width), NEG_BIG, _F32), jnp.ones((1, width), _F32),
                        jnp.max(s_new, axis=0, keepdims=True)))
    carries = step(0, 0, carries, has_pv=False, has_scores=True)

    def body(it, carries):
        for h in range(2):
            carries = step(2 * it + 1 + h, (h + 1) % 2, carries, has_pv=True, has_scores=True)
        return carries

    carries = lax.fori_loop(0, (n_kv - 2) // 2, body, carries)
    last = (n_kv - 1) % 2
    carries = step(n_kv - 1, last, carries, has_pv=True, has_scores=False)
    v_blk = vT_ref[0, 0, n_kv - 1]
    for i, (_, _, p_a, p_b, acc_ref) in enumerate(probs):
        _, alpha, _ = carries[i]
        acc = alpha * acc_ref[...] + jnp.dot(v_blk, (p_a, p_b)[last][...], preferred_element_type=_F32)
        o_t = acc[0:dv, :] / acc[dv:dv + 1, :]
        for g in range(group):
            o_ref[0, i * tqp:(i + 1) * tqp, g * dv:(g + 1) * dv] = (
                o_t[:, g * tqp:(g + 1) * tqp].T.astype(o_ref.dtype))


def _attention(qT, k, vT, *, tq):
    B, Hq, dk, S = qT.shape
    _, Hkv, n_kv, dv_ext, tc = vT.shape
    dv = dv_ext - V_EXTRA_ROWS
    group = Hq // Hkv
    width = group * tq // ATTN_PROBLEMS
    assert n_kv >= 2 and n_kv % 2 == 0 and width == MXU_DIM
    return pl.pallas_call(
        functools.partial(_attn_kernel, group=group, n_kv=n_kv),
        grid=(B, Hkv, S // tq),
        in_specs=[
            pl.BlockSpec((1, group, dk, tq), lambda b, h, q: (b, h, 0, q)),
            pl.BlockSpec((1, 1, S, dk), lambda b, h, q: (b, h, 0, 0), pipeline_mode=pl.Buffered(1)),
            pl.BlockSpec((1, 1, n_kv, dv_ext, tc), lambda b, h, q: (b, h, 0, 0, 0),
                         pipeline_mode=pl.Buffered(1)),
        ],
        out_specs=pl.BlockSpec((1, tq, group * dv), lambda b, h, q: (b, q, h)),
        out_shape=jax.ShapeDtypeStruct((B, S, Hq * dv), _BF16),
        scratch_shapes=[pltpu.VMEM((tc, width), _F32), pltpu.VMEM((tc, width), _F32),
                        pltpu.VMEM((tc, width), _BF16), pltpu.VMEM((tc, width), _BF16),
                        pltpu.VMEM((dv_ext, width), _F32)] * ATTN_PROBLEMS,
        compiler_params=pltpu.CompilerParams(
            dimension_semantics=("parallel", "parallel", "parallel"),
            vmem_limit_bytes=VMEM_LIMIT_BYTES),
        name="attn",
    )(qT, k, vT)


def _mlp_kernel(x_ref, oa_ref, ob_ref, wo_a_ref, wo_b_ref, g_mlp_ref, w_up_ref, w_dn_ref, g_fin_ref,
                y_ref, *, final):
    x1 = (x_ref[...]
          + jnp.dot(oa_ref[...], wo_a_ref[...], preferred_element_type=_F32)
          + jnp.dot(ob_ref[...], wo_b_ref[...], preferred_element_type=_F32))
    hn = _rms(x1, g_mlp_ref[...]).astype(_BF16)
    y_ref[...] = x1
    for c in range(D_FF // FF_CHUNK):
        u = jnp.dot(hn, w_up_ref[:, c * FF_CHUNK:(c + 1) * FF_CHUNK], preferred_element_type=_F32)
        a = jnp.square(jnp.maximum(u, 0.0)).astype(_BF16)
        y_ref[...] += jnp.dot(a, w_dn_ref[c * FF_CHUNK:(c + 1) * FF_CHUNK, :], preferred_element_type=_F32)
    if final:
        y_ref[...] = _rms(y_ref[...], g_fin_ref[...])


def _single_buffered(shape):
    nd = len(shape)
    return pl.BlockSpec(shape, lambda *_: (0,) * nd, pipeline_mode=pl.Buffered(1))


def _mlp(x, oa, ob, wo_a, wo_b, g_mlp, w_up, w_dn, g_fin, *, final):
    B, S, D = x.shape
    tm = MLP_TM
    tok = lambda w: pl.BlockSpec((None, tm, w), lambda b, s: (b, s, 0))
    return pl.pallas_call(
        functools.partial(_mlp_kernel, final=final),
        grid=(B, S // tm),
        in_specs=[
            tok(D), tok(oa.shape[-1]), tok(ob.shape[-1]),
            _single_buffered(wo_a.shape), _single_buffered(wo_b.shape), _const_spec(g_mlp.shape),
            _single_buffered(w_up.shape), _single_buffered(w_dn.shape), _const_spec(g_fin.shape),
        ],
        out_specs=tok(D),
        out_shape=jax.ShapeDtypeStruct((B, S, D), _F32),
        compiler_params=pltpu.CompilerParams(
            dimension_semantics=("parallel", "parallel"), vmem_limit_bytes=VMEM_LIMIT_BYTES),
        name="mlp",
    )(x, oa, ob, wo_a, wo_b, g_mlp, w_up, w_dn, g_fin)


def _paired_layout(width):
    q = width // 4
    lane = np.arange(LANES)
    side, r = lane // (LANES // 2), lane % (LANES // 2)
    axis, j = r // q, r % q
    src = np.where(axis < 2, axis * 2 * q + side * q + j, -1)
    return src, j, axis == 0, side == 0


def _take_cols(w, src):
    w_ext = jnp.concatenate([w, jnp.zeros(w.shape[:-1] + (1,), w.dtype)], axis=-1)
    return jnp.take(w_ext, np.where(src < 0, w.shape[-1], src), axis=-1)


def _rotary_tables(S):
    n_rows = S // GRID_W
    row = jnp.arange(n_rows, dtype=_F32)[:, None]
    col = jnp.arange(GRID_W, dtype=_F32)[:, None]
    out = []
    for width in (GQA_HEAD_DIM, MLA_ROPE):
        _, j, by_row, x1_side = _paired_layout(width)
        freqs = ROPE_THETA ** (-(2.0 * j.astype(np.float32)) / (width // 2))
        for fn, sign in ((jnp.cos, 1.0), (jnp.sin, np.where(x1_side, -1.0, 1.0).astype(np.float32))):
            by_r = jnp.broadcast_to((fn(row * freqs) * sign)[:, None, :], (n_rows, GRID_W, LANES))
            by_c = jnp.broadcast_to((fn(col * freqs) * sign)[None, :, :], (n_rows, GRID_W, LANES))
            out.append(jnp.where(by_row, by_r, by_c).reshape(S, LANES))
    return jnp.concatenate(out, axis=1)


def _relayout_weights(w_in, w_mla_q_up):
    o = [0]
    for w in (MLA_Q_LORA, MLA_KV_LORA, MLA_ROPE, GQA_HEADS * GQA_HEAD_DIM,
              GQA_KV_HEADS * GQA_HEAD_DIM, GQA_KV_HEADS * GQA_HEAD_DIM):
        o.append(o[-1] + w)
    cq, ckv, kr, qb, kb, vb = (w_in[..., o[i]:o[i + 1]] for i in range(6))
    src_g = _paired_layout(GQA_HEAD_DIM)[0]
    src_m = _paired_layout(MLA_ROPE)[0]
    heads = lambda w, n: [_take_cols(w[..., h * GQA_HEAD_DIM:(h + 1) * GQA_HEAD_DIM], src_g) for h in range(n)]
    w_in_p = jnp.concatenate([cq, ckv] + heads(qb, GQA_HEADS) + heads(kb, GQA_KV_HEADS)
                             + [vb, _take_cols(kr, src_m)], axis=-1).astype(_BF16)
    L, R, _ = w_mla_q_up.shape
    wq = w_mla_q_up.reshape(L, R, MLA_HEADS, MLA_NOPE + MLA_ROPE)
    wq = jnp.concatenate([wq[..., :MLA_NOPE], _take_cols(wq[..., MLA_NOPE:], src_m)], axis=-1)
    return w_in_p, wq.reshape(L, R, MLA_HEADS * MLA_DK).astype(_BF16)


def _trunk(x, p, tab):
    for l in range(DEPTH):
        qaT, ka, vaT, qbT, kb, vbT = _project(
            x, p["attn_norm"][l], p["w_in"][l], p["mla_q_norm"][l], p["w_qup"][l],
            p["mla_kv_norm"][l], p["w_kvup"][l], p["gqa_q_norm"][l], p["gqa_k_norm"][l], tab)
        oa = _attention(qaT, ka, vaT, tq=MLA_TQ)
        ob = _attention(qbT, kb, vbT, tq=GQA_TQ)
        x = _mlp(x, oa, ob, p["wo_a"][l], p["wo_b"][l], p["mlp_norm"][l], p["w_up"][l], p["w_dn"][l],
                 p["final_norm"], final=(l == DEPTH - 1))
    return x


def kernel(x_prompt, x_sample, attn_norm, w_in, mla_q_norm, w_mla_q_up, mla_kv_norm, w_mla_kv_up,
           gqa_q_norm, gqa_k_norm, w_out, mlp_norm, w_mlp_up, w_mlp_down, final_norm):
    w_in_p, w_qup = _relayout_weights(w_in, w_mla_q_up)
    row = lambda g: g[:, None, :]
    split = MLA_HEADS * MLA_V
    p = {
        "attn_norm": row(attn_norm), "w_in": w_in_p, "mla_q_norm": row(mla_q_norm), "w_qup": w_qup,
        "mla_kv_norm": row(mla_kv_norm), "w_kvup": w_mla_kv_up.astype(_BF16),
        "gqa_q_norm": row(_take_cols(gqa_q_norm, _paired_layout(GQA_HEAD_DIM)[0])),
        "gqa_k_norm": row(_take_cols(gqa_k_norm, _paired_layout(GQA_HEAD_DIM)[0])),
        "wo_a": w_out[:, :split, :].astype(_BF16), "wo_b": w_out[:, split:, :].astype(_BF16),
        "mlp_norm": row(mlp_norm), "w_up": w_mlp_up.astype(_BF16), "w_dn": w_mlp_down.astype(_BF16),
        "final_norm": final_norm[None, :],
    }
    tab = _rotary_tables(max(x_prompt.shape[1], x_sample.shape[1]))
    return _trunk(x_prompt, p, tab), _trunk(x_sample, p, tab)
```

```python
import functools
import math

import jax
import jax.numpy as jnp
import numpy as np
from jax import lax
from jax.experimental import pallas as pl
from jax.experimental.pallas import tpu as pltpu

D_MODEL = 1024
GRID_W = 64
ROPE_THETA = 10000.0
NORM_EPS = 1e-6
MLA_HEADS = 4
MLA_Q_LORA = 384
MLA_KV_LORA = 256
MLA_NOPE = 128
MLA_ROPE = 64
MLA_V = 128
GQA_HEADS = 4
GQA_KV_HEADS = 2
GQA_HEAD_DIM = 128
D_FF = 4 * D_MODEL
DEPTH = 2

LANES = 128
MXU_DIM = 256
VMEM_LIMIT_BYTES = 56 * 1024 * 1024

PROJ_TM = 1024
PROJ_CHAIN = 256
MLP_TM = 512
KV_CHUNK = 512
ATTN_PROBLEMS = 16
MLA_TQ = 4096
GQA_TQ = 2048
FF_CHUNK = 1024

MLA_DK = 2 * LANES
V_EXTRA_ROWS = 16
LOG2E = math.log2(math.e)
NEG_BIG = -1e30

_BF16 = jnp.bfloat16
_F32 = jnp.float32


def _rms(x, g):
    ms = jnp.mean(x * x, axis=-1, keepdims=True)
    return x * lax.rsqrt(ms + NORM_EPS) * g


def _rotary(x, cos, sin):
    return x * cos + pltpu.roll(x, LANES // 2, 1) * sin


def _proj_kernel(x_ref, g_attn_ref, w_in_ref, g_q_ref, w_qup_ref, g_kv_ref, w_kvup_ref,
                 g_gq_ref, g_gk_ref, tab_ref,
                 qaT_ref, ka_ref, vaT_ref, qbT_ref, kb_ref, vbT_ref, *, n_chunks):
    tc = KV_CHUNK
    o_ckv = MLA_Q_LORA
    o_qb = o_ckv + MLA_KV_LORA
    o_kb = o_qb + GQA_HEADS * GQA_HEAD_DIM
    o_vb = o_kb + GQA_KV_HEADS * GQA_HEAD_DIM
    o_kr = o_vb + GQA_KV_HEADS * GQA_HEAD_DIM
    g_q = g_q_ref[...] * ((MLA_NOPE + MLA_ROPE) ** -0.5 * LOG2E)
    ch = PROJ_CHAIN
    ones_rows = jnp.where(lax.broadcasted_iota(jnp.int32, (V_EXTRA_ROWS, ch), 0) == 0,
                          1.0, 0.0).astype(_BF16)
    g_gq = g_gq_ref[...] * (GQA_HEAD_DIM ** -0.5 * LOG2E)
    g_gk = g_gk_ref[...]

    for c in range(n_chunks * tc // ch):
        rows = slice(c * ch, (c + 1) * ch)
        chunk, cols = (c * ch) // tc, slice((c * ch) % tc, (c * ch) % tc + ch)
        h = _rms(x_ref[rows, :], g_attn_ref[...])
        z = jnp.dot(h.astype(_BF16), w_in_ref[...], preferred_element_type=_F32)
        tab = tab_ref[rows, :]
        rope_g = functools.partial(_rotary, cos=tab[:, 0:LANES], sin=tab[:, LANES:2 * LANES])
        rope_m = functools.partial(_rotary, cos=tab[:, 2 * LANES:3 * LANES], sin=tab[:, 3 * LANES:4 * LANES])

        def store_t(ref, head, val):
            dv = val.shape[1]
            ref[0, head, chunk, 0:dv, cols] = val.T.astype(_BF16)
            ref[0, head, chunk, dv:dv + V_EXTRA_ROWS, cols] = ones_rows

        qa = jnp.dot(_rms(z[:, :o_ckv], g_q).astype(_BF16), w_qup_ref[...],
                     preferred_element_type=_F32)
        kva = jnp.dot(_rms(z[:, o_ckv:o_qb], g_kv_ref[...]).astype(_BF16), w_kvup_ref[...],
                      preferred_element_type=_F32)
        k_rope = rope_m(z[:, o_kr:o_kr + LANES]).astype(_BF16)
        for hd in range(MLA_HEADS):
            base = hd * MLA_DK
            q_nope = qa[:, base:base + LANES]
            q_rope = rope_m(qa[:, base + LANES:base + 2 * LANES])
            qaT_ref[0, hd, 0:LANES, rows] = q_nope.T.astype(_BF16)
            qaT_ref[0, hd, LANES:2 * LANES, rows] = q_rope.T.astype(_BF16)
            ka_ref[0, hd, rows, 0:LANES] = kva[:, base:base + LANES].astype(_BF16)
            ka_ref[0, hd, rows, LANES:2 * LANES] = k_rope
            store_t(vaT_ref, hd, kva[:, base + LANES:base + 2 * LANES])

        for hd in range(GQA_HEADS):
            qh = z[:, o_qb + hd * LANES:o_qb + (hd + 1) * LANES]
            qbT_ref[0, hd, :, rows] = rope_g(_rms(qh, g_gq)).T.astype(_BF16)
        for hd in range(GQA_KV_HEADS):
            kh = z[:, o_kb + hd * LANES:o_kb + (hd + 1) * LANES]
            kb_ref[0, hd, rows, :] = rope_g(_rms(kh, g_gk)).astype(_BF16)
            store_t(vbT_ref, hd, z[:, o_vb + hd * LANES:o_vb + (hd + 1) * LANES])


def _const_spec(shape):
    nd = len(shape)
    return pl.BlockSpec(shape, lambda *_: (0,) * nd)


def _project(x, g_attn, w_in, g_q, w_qup, g_kv, w_kvup, g_gq, g_gk, tab):
    B, S, D = x.shape
    tm = PROJ_TM
    n_chunks = tm // KV_CHUNK
    assert S % tm == 0 and tm % KV_CHUNK == 0 and KV_CHUNK % PROJ_CHAIN == 0
    ns = S // tm
    nc = S // KV_CHUNK
    out_shape = (
        jax.ShapeDtypeStruct((B, MLA_HEADS, MLA_DK, S), _BF16),
        jax.ShapeDtypeStruct((B, MLA_HEADS, S, MLA_DK), _BF16),
        jax.ShapeDtypeStruct((B, MLA_HEADS, nc, MLA_V + V_EXTRA_ROWS, KV_CHUNK), _BF16),
        jax.ShapeDtypeStruct((B, GQA_HEADS, GQA_HEAD_DIM, S), _BF16),
        jax.ShapeDtypeStruct((B, GQA_KV_HEADS, S, GQA_HEAD_DIM), _BF16),
        jax.ShapeDtypeStruct((B, GQA_KV_HEADS, nc, GQA_HEAD_DIM + V_EXTRA_ROWS, KV_CHUNK), _BF16),
    )
    in_specs = [
        pl.BlockSpec((None, tm, D), lambda b, s: (b, s, 0)),
        _const_spec(g_attn.shape), _const_spec(w_in.shape), _const_spec(g_q.shape),
        _const_spec(w_qup.shape), _const_spec(g_kv.shape), _const_spec(w_kvup.shape),
        _const_spec(g_gq.shape), _const_spec(g_gk.shape),
        pl.BlockSpec((tm, tab.shape[1]), lambda b, s: (s, 0)),
    ]
    out_specs = (
        pl.BlockSpec((1, MLA_HEADS, MLA_DK, tm), lambda b, s: (b, 0, 0, s)),
        pl.BlockSpec((1, MLA_HEADS, tm, MLA_DK), lambda b, s: (b, 0, s, 0)),
        pl.BlockSpec((1, MLA_HEADS, n_chunks, MLA_V + V_EXTRA_ROWS, KV_CHUNK), lambda b, s: (b, 0, s, 0, 0)),
        pl.BlockSpec((1, GQA_HEADS, GQA_HEAD_DIM, tm), lambda b, s: (b, 0, 0, s)),
        pl.BlockSpec((1, GQA_KV_HEADS, tm, GQA_HEAD_DIM), lambda b, s: (b, 0, s, 0)),
        pl.BlockSpec((1, GQA_KV_HEADS, n_chunks, GQA_HEAD_DIM + V_EXTRA_ROWS, KV_CHUNK),
                     lambda b, s: (b, 0, s, 0, 0)),
    )
    return pl.pallas_call(
        functools.partial(_proj_kernel, n_chunks=n_chunks),
        grid=(B, ns),
        in_specs=in_specs,
        out_specs=out_specs,
        out_shape=out_shape,
        compiler_params=pltpu.CompilerParams(
            dimension_semantics=("parallel", "parallel"), vmem_limit_bytes=VMEM_LIMIT_BYTES),
        name="proj",
    )(x, g_attn, w_in, g_q, w_qup, g_kv, w_kvup, g_gq, g_gk, tab)


def _attn_kernel(qT_ref, k_ref, vT_ref, o_ref, *scratch, group, n_kv):
    tc = KV_CHUNK
    dv = vT_ref.shape[3] - V_EXTRA_ROWS
    n_prob = len(scratch) // 5
    tqp = qT_ref.shape[3] // n_prob
    width = group * tqp
    probs = [scratch[5 * i:5 * i + 5] for i in range(n_prob)]
    q_ts = [jnp.concatenate([qT_ref[0, g, :, i * tqp:(i + 1) * tqp] for g in range(group)], axis=1)
            for i in range(n_prob)]

    HALVES = (slice(0, tc // 2), slice(tc // 2, tc))

    def k_rows(j):
        return k_ref[0, 0, pl.ds(pl.multiple_of(j * tc, tc), tc), :]

    def step(j, parity, carries, *, has_pv, has_scores):
        k_blk = k_rows(j + 1) if has_scores else None
        v_blk = vT_ref[0, 0, j - 1] if has_pv else None
        issued = []
        for q_t, (_, _, p_a, p_b, _) in zip(q_ts, probs):
            s_new = jnp.dot(k_blk, q_t, preferred_element_type=_F32) if has_scores else None
            pv = (jnp.dot(v_blk, (p_a, p_b)[1 - parity][...], preferred_element_type=_F32)
                  if has_pv else None)
            issued.append((s_new, pv))
        out = []
        for (s_new, pv), (m, alpha_prev, c_cur), (s_a, s_b, p_a, p_b, acc_ref) in zip(issued, carries, probs):
            m_new = jnp.maximum(m, c_cur)
            alpha = jnp.exp2(m - m_new)
            for rows in HALVES:
                x = ((s_a, s_b)[parity][rows, :] - m_new).astype(_BF16)
                (p_a, p_b)[parity][rows, :] = jnp.exp2(x)
                if has_scores:
                    (s_a, s_b)[1 - parity][rows, :] = s_new[rows, :]
            c_nxt = jnp.max(s_new, axis=0, keepdims=True) if has_scores else None
            if has_pv:
                acc_ref[...] = alpha_prev * acc_ref[...] + pv
            out.append((m_new, alpha, c_nxt))
        return out

    carries = []
    k_blk = k_rows(0)
    for q_t, (s_a, _, _, _, acc_ref) in zip(q_ts, probs):
        acc_ref[...] = jnp.zeros_like(acc_ref)
        s_new = jnp.dot(k_blk, q_t, preferred_element_type=_F32)
        s_a[...] = s_new
        carries.append((jnp.full((1, width), NEG_BIG, _F32), jnp.ones((1, width), _F32),
                        jnp.max(s_new, axis=0, keepdims=True)))
    carries = step(0, 0, carries, has_pv=False, has_scores=True)

    def body(it, carries):
        for h in range(2):
            carries = step(2 * it + 1 + h, (h + 1) % 2, carries, has_pv=True, has_scores=True)
        return carries

    carries = lax.fori_loop(0, (n_kv - 2) // 2, body, carries)
    last = (n_kv - 1) % 2
    carries = step(n_kv - 1, last, carries, has_pv=True, has_scores=False)
    v_blk = vT_ref[0, 0, n_kv - 1]
    for i, (_, _, p_a, p_b, acc_ref) in enumerate(probs):
        _, alpha, _ = carries[i]
        acc = alpha * acc_ref[...] + jnp.dot(v_blk, (p_a, p_b)[last][...], preferred_element_type=_F32)
        o_t = acc[0:dv, :] / acc[dv:dv + 1, :]
        for g in range(group):
            o_ref[0, i * tqp:(i + 1) * tqp, g * dv:(g + 1) * dv] = (
                o_t[:, g * tqp:(g + 1) * tqp].T.astype(o_ref.dtype))


def _attention(qT, k, vT, *, tq):
    B, Hq, dk, S = qT.shape
    _, Hkv, n_kv, dv_ext, tc = vT.shape
    dv = dv_ext - V_EXTRA_ROWS
    group = Hq // Hkv
    width = group * tq // ATTN_PROBLEMS
    assert S % tq == 0 and n_kv >= 2 and n_kv % 2 == 0 and width == MXU_DIM
    return pl.pallas_call(
        functools.partial(_attn_kernel, group=group, n_kv=n_kv),
        grid=(B, Hkv, S // tq),
        in_specs=[
            pl.BlockSpec((1, group, dk, tq), lambda b, h, q: (b, h, 0, q)),
            pl.BlockSpec((1, 1, S, dk), lambda b, h, q: (b, h, 0, 0), pipeline_mode=pl.Buffered(1)),
            pl.BlockSpec((1, 1, n_kv, dv_ext, tc), lambda b, h, q: (b, h, 0, 0, 0),
                         pipeline_mode=pl.Buffered(1)),
        ],
        out_specs=pl.BlockSpec((1, tq, group * dv), lambda b, h, q: (b, q, h)),
        out_shape=jax.ShapeDtypeStruct((B, S, Hq * dv), _BF16),
        scratch_shapes=[pltpu.VMEM((tc, width), _F32), pltpu.VMEM((tc, width), _F32),
                        pltpu.VMEM((tc, width), _BF16), pltpu.VMEM((tc, width), _BF16),
                        pltpu.VMEM((dv_ext, width), _F32)] * ATTN_PROBLEMS,
        compiler_params=pltpu.CompilerParams(
            dimension_semantics=("parallel", "parallel", "parallel"),
            vmem_limit_bytes=VMEM_LIMIT_BYTES),
        name="attn",
    )(qT, k, vT)


def _mlp_kernel(x_ref, oa_ref, ob_ref, wo_a_ref, wo_b_ref, g_mlp_ref, w_up_ref, w_dn_ref, g_fin_ref,
                y_ref, *, final):
    x1 = (x_ref[...]
          + jnp.dot(oa_ref[...], wo_a_ref[...], preferred_element_type=_F32)
          + jnp.dot(ob_ref[...], wo_b_ref[...], preferred_element_type=_F32))
    hn = _rms(x1, g_mlp_ref[...]).astype(_BF16)
    y_ref[...] = x1
    for c in range(D_FF // FF_CHUNK):
        u = jnp.dot(hn, w_up_ref[:, c * FF_CHUNK:(c + 1) * FF_CHUNK], preferred_element_type=_F32)
        a = jnp.square(jnp.maximum(u, 0.0)).astype(_BF16)
        y_ref[...] += jnp.dot(a, w_dn_ref[c * FF_CHUNK:(c + 1) * FF_CHUNK, :], preferred_element_type=_F32)
    if final:
        y_ref[...] = _rms(y_ref[...], g_fin_ref[...])


def _single_buffered(shape):
    nd = len(shape)
    return pl.BlockSpec(shape, lambda *_: (0,) * nd, pipeline_mode=pl.Buffered(1))


def _mlp(x, oa, ob, wo_a, wo_b, g_mlp, w_up, w_dn, g_fin, *, final):
    B, S, D = x.shape
    tm = MLP_TM
    assert S % tm == 0
    tok = lambda w: pl.BlockSpec((None, tm, w), lambda b, s: (b, s, 0))
    return pl.pallas_call(
        functools.partial(_mlp_kernel, final=final),
        grid=(B, S // tm),
        in_specs=[
            tok(D), tok(oa.shape[-1]), tok(ob.shape[-1]),
            _single_buffered(wo_a.shape), _single_buffered(wo_b.shape), _const_spec(g_mlp.shape),
            _single_buffered(w_up.shape), _single_buffered(w_dn.shape), _const_spec(g_fin.shape),
        ],
        out_specs=tok(D),
        out_shape=jax.ShapeDtypeStruct((B, S, D), _F32),
        compiler_params=pltpu.CompilerParams(
            dimension_semantics=("parallel", "parallel"), vmem_limit_bytes=VMEM_LIMIT_BYTES),
        name="mlp",
    )(x, oa, ob, wo_a, wo_b, g_mlp, w_up, w_dn, g_fin)


def _paired_layout(width):
    q = width // 4
    lane = np.arange(LANES)
    side, r = lane // (LANES // 2), lane % (LANES // 2)
    axis, j = r // q, r % q
    src = np.where(axis < 2, axis * 2 * q + side * q + j, -1)
    return src, j, axis == 0, side == 0


def _take_cols(w, src):
    w_ext = jnp.concatenate([w, jnp.zeros(w.shape[:-1] + (1,), w.dtype)], axis=-1)
    return jnp.take(w_ext, np.where(src < 0, w.shape[-1], src), axis=-1)


def _rotary_tables(S):
    n_rows = S // GRID_W
    row = jnp.arange(n_rows, dtype=_F32)[:, None]
    col = jnp.arange(GRID_W, dtype=_F32)[:, None]
    out = []
    for width in (GQA_HEAD_DIM, MLA_ROPE):
        _, j, by_row, x1_side = _paired_layout(width)
        freqs = ROPE_THETA ** (-(2.0 * j.astype(np.float32)) / (width // 2))
        for fn, sign in ((jnp.cos, 1.0), (jnp.sin, np.where(x1_side, -1.0, 1.0).astype(np.float32))):
            by_r = jnp.broadcast_to((fn(row * freqs) * sign)[:, None, :], (n_rows, GRID_W, LANES))
            by_c = jnp.broadcast_to((fn(col * freqs) * sign)[None, :, :], (n_rows, GRID_W, LANES))
            out.append(jnp.where(by_row, by_r, by_c).reshape(S, LANES))
    return jnp.concatenate(out, axis=1)


def _relayout_weights(w_in, w_mla_q_up):
    o = [0]
    for w in (MLA_Q_LORA, MLA_KV_LORA, MLA_ROPE, GQA_HEADS * GQA_HEAD_DIM,
              GQA_KV_HEADS * GQA_HEAD_DIM, GQA_KV_HEADS * GQA_HEAD_DIM):
        o.append(o[-1] + w)
    cq, ckv, kr, qb, kb, vb = (w_in[..., o[i]:o[i + 1]] for i in range(6))
    src_g = _paired_layout(GQA_HEAD_DIM)[0]
    src_m = _paired_layout(MLA_ROPE)[0]
    heads = lambda w, n: [_take_cols(w[..., h * GQA_HEAD_DIM:(h + 1) * GQA_HEAD_DIM], src_g) for h in range(n)]
    w_in_p = jnp.concatenate([cq, ckv] + heads(qb, GQA_HEADS) + heads(kb, GQA_KV_HEADS)
                             + [vb, _take_cols(kr, src_m)], axis=-1).astype(_BF16)
    L, R, _ = w_mla_q_up.shape
    wq = w_mla_q_up.reshape(L, R, MLA_HEADS, MLA_NOPE + MLA_ROPE)
    wq = jnp.concatenate([wq[..., :MLA_NOPE], _take_cols(wq[..., MLA_NOPE:], src_m)], axis=-1)
    return w_in_p, wq.reshape(L, R, MLA_HEADS * MLA_DK).astype(_BF16)


def _trunk(x, p, tab):
    for l in range(DEPTH):
        qaT, ka, vaT, qbT, kb, vbT = _project(
            x, p["attn_norm"][l], p["w_in"][l], p["mla_q_norm"][l], p["w_qup"][l],
            p["mla_kv_norm"][l], p["w_kvup"][l], p["gqa_q_norm"][l], p["gqa_k_norm"][l], tab)
        oa = _attention(qaT, ka, vaT, tq=MLA_TQ)
        ob = _attention(qbT, kb, vbT, tq=GQA_TQ)
        x = _mlp(x, oa, ob, p["wo_a"][l], p["wo_b"][l], p["mlp_norm"][l], p["w_up"][l], p["w_dn"][l],
                 p["final_norm"], final=(l == DEPTH - 1))
    return x


def kernel(x_prompt, x_sample, attn_norm, w_in, mla_q_norm, w_mla_q_up, mla_kv_norm, w_mla_kv_up,
           gqa_q_norm, gqa_k_norm, w_out, mlp_norm, w_mlp_up, w_mlp_down, final_norm):
    w_in_p, w_qup = _relayout_weights(w_in, w_mla_q_up)
    row = lambda g: g[:, None, :]
    split = MLA_HEADS * MLA_V
    p = {
        "attn_norm": row(attn_norm), "w_in": w_in_p, "mla_q_norm": row(mla_q_norm), "w_qup": w_qup,
        "mla_kv_norm": row(mla_kv_norm), "w_kvup": w_mla_kv_up.astype(_BF16),
        "gqa_q_norm": row(_take_cols(gqa_q_norm, _paired_layout(GQA_HEAD_DIM)[0])),
        "gqa_k_norm": row(_take_cols(gqa_k_norm, _paired_layout(GQA_HEAD_DIM)[0])),
        "wo_a": w_out[:, :split, :].astype(_BF16), "wo_b": w_out[:, split:, :].astype(_BF16),
        "mlp_norm": row(mlp_norm), "w_up": w_mlp_up.astype(_BF16), "w_dn": w_mlp_down.astype(_BF16),
        "final_norm": final_norm[None, :],
    }
    tab = _rotary_tables(max(x_prompt.shape[1], x_sample.shape[1]))
    return _trunk(x_prompt, p, tab), _trunk(x_sample, p, tab)
```

```python
import functools
import math

import jax
import jax.numpy as jnp
import numpy as np
from jax import lax
from jax.experimental import pallas as pl
from jax.experimental.pallas import tpu as pltpu

D_MODEL = 1024
GRID_W = 64
ROPE_THETA = 10000.0
NORM_EPS = 1e-6
MLA_HEADS = 4
MLA_Q_LORA = 384
MLA_KV_LORA = 256
MLA_NOPE = 128
MLA_ROPE = 64
MLA_V = 128
GQA_HEADS = 4
GQA_KV_HEADS = 2
GQA_HEAD_DIM = 128
D_FF = 4 * D_MODEL
DEPTH = 2

LANES = 128
MXU_DIM = 256
VMEM_LIMIT_BYTES = 56 * 1024 * 1024

PROJ_TM = 1024
PROJ_CHAIN = 256
MLP_TM = 512
KV_CHUNK = 512
ATTN_PROBLEMS = 16
MLA_TQ = 4096
GQA_TQ = 2048
FF_CHUNK = 1024

MLA_DK = 2 * LANES
V_EXTRA_ROWS = 16
LOG2E = math.log2(math.e)
NEG_BIG = -1e30

_BF16 = jnp.bfloat16
_F32 = jnp.float32


def _rms(x, g):
    ms = jnp.mean(x * x, axis=-1, keepdims=True)
    return x * lax.rsqrt(ms + NORM_EPS) * g


def _rotary(x, cos, sin):
    return x * cos + pltpu.roll(x, LANES // 2, 1) * sin


def _proj_kernel(x_ref, g_attn_ref, w_in_ref, g_q_ref, w_qup_ref, g_kv_ref, w_kvup_ref,
                 g_gq_ref, g_gk_ref, tab_ref,
                 qaT_ref, ka_ref, vaT_ref, qbT_ref, kb_ref, vbT_ref, *, n_chunks):
    tc = KV_CHUNK
    o_ckv = MLA_Q_LORA
    o_qb = o_ckv + MLA_KV_LORA
    o_kb = o_qb + GQA_HEADS * GQA_HEAD_DIM
    o_vb = o_kb + GQA_KV_HEADS * GQA_HEAD_DIM
    o_kr = o_vb + GQA_KV_HEADS * GQA_HEAD_DIM
    g_q = g_q_ref[...] * ((MLA_NOPE + MLA_ROPE) ** -0.5 * LOG2E)
    ch = PROJ_CHAIN
    ones_rows = jnp.where(lax.broadcasted_iota(jnp.int32, (V_EXTRA_ROWS, ch), 0) == 0,
                          1.0, 0.0).astype(_BF16)
    g_gq = g_gq_ref[...] * (GQA_HEAD_DIM ** -0.5 * LOG2E)
    g_gk = g_gk_ref[...]

    for c in range(n_chunks * tc // ch):
        rows = slice(c * ch, (c + 1) * ch)
        chunk, cols = (c * ch) // tc, slice((c * ch) % tc, (c * ch) % tc + ch)
        h = _rms(x_ref[rows, :], g_attn_ref[...])
        z = jnp.dot(h.astype(_BF16), w_in_ref[...], preferred_element_type=_F32)
        tab = tab_ref[rows, :]
        rope_g = functools.partial(_rotary, cos=tab[:, 0:LANES], sin=tab[:, LANES:2 * LANES])
        rope_m = functools.partial(_rotary, cos=tab[:, 2 * LANES:3 * LANES], sin=tab[:, 3 * LANES:4 * LANES])

        def store_t(ref, head, val):
            dv = val.shape[1]
            ref[0, head, chunk, 0:dv, cols] = val.T.astype(_BF16)
            ref[0, head, chunk, dv:dv + V_EXTRA_ROWS, cols] = ones_rows

        qa = jnp.dot(_rms(z[:, :o_ckv], g_q).astype(_BF16), w_qup_ref[...],
                     preferred_element_type=_F32)
        kva = jnp.dot(_rms(z[:, o_ckv:o_qb], g_kv_ref[...]).astype(_BF16), w_kvup_ref[...],
                      preferred_element_type=_F32)
        k_rope = rope_m(z[:, o_kr:o_kr + LANES]).astype(_BF16)
        for hd in range(MLA_HEADS):
            base = hd * MLA_DK
            q_nope = qa[:, base:base + LANES]
            q_rope = rope_m(qa[:, base + LANES:base + 2 * LANES])
            qaT_ref[0, hd, 0:LANES, rows] = q_nope.T.astype(_BF16)
            qaT_ref[0, hd, LANES:2 * LANES, rows] = q_rope.T.astype(_BF16)
            ka_ref[0, hd, rows, 0:LANES] = kva[:, base:base + LANES].astype(_BF16)
            ka_ref[0, hd, rows, LANES:2 * LANES] = k_rope
            store_t(vaT_ref, hd, kva[:, base + LANES:base + 2 * LANES])

        for hd in range(GQA_HEADS):
            qh = z[:, o_qb + hd * LANES:o_qb + (hd + 1) * LANES]
            qbT_ref[0, hd, :, rows] = rope_g(_rms(qh, g_gq)).T.astype(_BF16)
        for hd in range(GQA_KV_HEADS):
            kh = z[:, o_kb + hd * LANES:o_kb + (hd + 1) * LANES]
            kb_ref[0, hd, rows, :] = rope_g(_rms(kh, g_gk)).astype(_BF16)
            store_t(vbT_ref, hd, z[:, o_vb + hd * LANES:o_vb + (hd + 1) * LANES])


def _const_spec(shape):
    nd = len(shape)
    return pl.BlockSpec(shape, lambda *_: (0,) * nd)


def _project(x, g_attn, w_in, g_q, w_qup, g_kv, w_kvup, g_gq, g_gk, tab):
    B, S, D = x.shape
    tm = PROJ_TM
    n_chunks = tm // KV_CHUNK
    assert S % tm == 0 and tm % KV_CHUNK == 0 and KV_CHUNK % PROJ_CHAIN == 0
    ns = S // tm
    nc = S // KV_CHUNK
    out_shape = (
        jax.ShapeDtypeStruct((B, MLA_HEADS, MLA_DK, S), _BF16),
        jax.ShapeDtypeStruct((B, MLA_HEADS, S, MLA_DK), _BF16),
        jax.ShapeDtypeStruct((B, MLA_HEADS, nc, MLA_V + V_EXTRA_ROWS, KV_CHUNK), _BF16),
        jax.ShapeDtypeStruct((B, GQA_HEADS, GQA_HEAD_DIM, S), _BF16),
        jax.ShapeDtypeStruct((B, GQA_KV_HEADS, S, GQA_HEAD_DIM), _BF16),
        jax.ShapeDtypeStruct((B, GQA_KV_HEADS, nc, GQA_HEAD_DIM + V_EXTRA_ROWS, KV_CHUNK), _BF16),
    )
    in_specs = [
        pl.BlockSpec((None, tm, D), lambda b, s: (b, s, 0)),
        _const_spec(g_attn.shape), _const_spec(w_in.shape), _const_spec(g_q.shape),
        _const_spec(w_qup.shape), _const_spec(g_kv.shape), _const_spec(w_kvup.shape),
        _const_spec(g_gq.shape), _const_spec(g_gk.shape),
        pl.BlockSpec((tm, tab.shape[1]), lambda b, s: (s, 0)),
    ]
    out_specs = (
        pl.BlockSpec((1, MLA_HEADS, MLA_DK, tm), lambda b, s: (b, 0, 0, s)),
        pl.BlockSpec((1, MLA_HEADS, tm, MLA_DK), lambda b, s: (b, 0, s, 0)),
        pl.BlockSpec((1, MLA_HEADS, n_chunks, MLA_V + V_EXTRA_ROWS, KV_CHUNK), lambda b, s: (b, 0, s, 0, 0)),
        pl.BlockSpec((1, GQA_HEADS, GQA_HEAD_DIM, tm), lambda b, s: (b, 0, 0, s)),
        pl.BlockSpec((1, GQA_KV_HEADS, tm, GQA_HEAD_DIM), lambda b, s: (b, 0, s, 0)),
        pl.BlockSpec((1, GQA_KV_HEADS, n_chunks, GQA_HEAD_DIM + V_EXTRA_ROWS, KV_CHUNK),
                     lambda b, s: (b, 0, s, 0, 0)),
    )
    return pl.pallas_call(
        functools.partial(_proj_kernel, n_chunks=n_chunks),
        grid=(B, ns),
        in_specs=in_specs,
        out_specs=out_specs,
        out_shape=out_shape,
        compiler_params=pltpu.CompilerParams(
            dimension_semantics=("parallel", "parallel"), vmem_limit_bytes=VMEM_LIMIT_BYTES),
        name="proj",
    )(x, g_attn, w_in, g_q, w_qup, g_kv, w_kvup, g_gq, g_gk, tab)


def _attn_kernel(qT_ref, k_ref, vT_ref, o_ref, *scratch, group, n_kv):
    tc = KV_CHUNK
    dv = vT_ref.shape[3] - V_EXTRA_ROWS
    n_prob = len(scratch) // 3
    tqp = qT_ref.shape[3] // n_prob
    width = group * tqp
    probs = [scratch[3 * i:3 * i + 3] for i in range(n_prob)]
    q_ts = [jnp.concatenate([qT_ref[0, g, :, i * tqp:(i + 1) * tqp] for g in range(group)], axis=1)
            for i in range(n_prob)]

    HALVES = (slice(0, tc // 2), slice(tc // 2, tc))

    def k_rows(j):
        return k_ref[0, 0, pl.ds(pl.multiple_of(j * tc, tc), tc), :]

    def step(j, parity, carries, *, has_pv, has_scores):
        k_blk = k_rows(j + 1) if has_scores else None
        v_blk = vT_ref[0, 0, j - 1] if has_pv else None
        issued = []
        for q_t, (_, p2, _) in zip(q_ts, probs):
            s_new = jnp.dot(k_blk, q_t, preferred_element_type=_F32) if has_scores else None
            pv = jnp.dot(v_blk, p2[1 - parity], preferred_element_type=_F32) if has_pv else None
            issued.append((s_new, pv))
        out = []
        for (s_new, pv), (m, alpha_prev, c_cur), (s2, p2, acc_ref) in zip(issued, carries, probs):
            m_new = jnp.maximum(m, c_cur)
            alpha = jnp.exp2(m - m_new)
            for rows in HALVES:
                x = (s2[parity, rows, :] - m_new).astype(_BF16)
                p2[parity, rows, :] = jnp.exp2(x)
                if has_scores:
                    s2[1 - parity, rows, :] = s_new[rows, :]
            c_nxt = jnp.max(s_new, axis=0, keepdims=True) if has_scores else None
            if has_pv:
                acc_ref[...] = alpha_prev * acc_ref[...] + pv
            out.append((m_new, alpha, c_nxt))
        return out

    carries = []
    k_blk = k_rows(0)
    for q_t, (s2, _, acc_ref) in zip(q_ts, probs):
        acc_ref[...] = jnp.zeros_like(acc_ref)
        s_new = jnp.dot(k_blk, q_t, preferred_element_type=_F32)
        s2[0] = s_new
        carries.append((jnp.full((1, width), NEG_BIG, _F32), jnp.ones((1, width), _F32),
                        jnp.max(s_new, axis=0, keepdims=True)))
    carries = step(0, 0, carries, has_pv=False, has_scores=True)

    carries = lax.fori_loop(1, n_kv - 1,
                            lambda j, c: step(j, j & 1, c, has_pv=True, has_scores=True), carries)
    last = (n_kv - 1) % 2
    carries = step(n_kv - 1, last, carries, has_pv=True, has_scores=False)
    v_blk = vT_ref[0, 0, n_kv - 1]
    for i, (_, p2, acc_ref) in enumerate(probs):
        _, alpha, _ = carries[i]
        acc = alpha * acc_ref[...] + jnp.dot(v_blk, p2[last], preferred_element_type=_F32)
        o_t = acc[0:dv, :] / acc[dv:dv + 1, :]
        for g in range(group):
            o_ref[0, i * tqp:(i + 1) * tqp, g * dv:(g + 1) * dv] = (
                o_t[:, g * tqp:(g + 1) * tqp].T.astype(o_ref.dtype))


def _attention(qT, k, vT, *, tq):
    B, Hq, dk, S = qT.shape
    _, Hkv, n_kv, dv_ext, tc = vT.shape
    dv = dv_ext - V_EXTRA_ROWS
    group = Hq // Hkv
    width = group * tq // ATTN_PROBLEMS
    assert S % tq == 0 and n_kv >= 2 and n_kv % 2 == 0 and width == MXU_DIM
    return pl.pallas_call(
        functools.partial(_attn_kernel, group=group, n_kv=n_kv),
        grid=(B, Hkv, S // tq),
        in_specs=[
            pl.BlockSpec((1, group, dk, tq), lambda b, h, q: (b, h, 0, q)),
            pl.BlockSpec((1, 1, S, dk), lambda b, h, q: (b, h, 0, 0), pipeline_mode=pl.Buffered(1)),
            pl.BlockSpec((1, 1, n_kv, dv_ext, tc), lambda b, h, q: (b, h, 0, 0, 0),
                         pipeline_mode=pl.Buffered(1)),
        ],
        out_specs=pl.BlockSpec((1, tq, group * dv), lambda b, h, q: (b, q, h)),
        out_shape=jax.ShapeDtypeStruct((B, S, Hq * dv), _BF16),
        scratch_shapes=[pltpu.VMEM((2, tc, width), _F32), pltpu.VMEM((2, tc, width), _BF16),
                        pltpu.VMEM((dv_ext, width), _F32)] * ATTN_PROBLEMS,
        compiler_params=pltpu.CompilerParams(
            dimension_semantics=("parallel", "parallel", "parallel"),
            vmem_limit_bytes=VMEM_LIMIT_BYTES),
        name="attn",
    )(qT, k, vT)


def _mlp_kernel(x_ref, oa_ref, ob_ref, wo_a_ref, wo_b_ref, g_mlp_ref, w_up_ref, w_dn_ref, g_fin_ref,
                y_ref, *, final):
    x1 = (x_ref[...]
          + jnp.dot(oa_ref[...], wo_a_ref[...], preferred_element_type=_F32)
          + jnp.dot(ob_ref[...], wo_b_ref[...], preferred_element_type=_F32))
    hn = _rms(x1, g_mlp_ref[...]).astype(_BF16)
    y_ref[...] = x1
    for c in range(D_FF // FF_CHUNK):
        u = jnp.dot(hn, w_up_ref[:, c * FF_CHUNK:(c + 1) * FF_CHUNK], preferred_element_type=_F32)
        a = jnp.square(jnp.maximum(u, 0.0)).astype(_BF16)
        y_ref[...] += jnp.dot(a, w_dn_ref[c * FF_CHUNK:(c + 1) * FF_CHUNK, :], preferred_element_type=_F32)
    if final:
        y_ref[...] = _rms(y_ref[...], g_fin_ref[...])


def _single_buffered(shape):
    nd = len(shape)
    return pl.BlockSpec(shape, lambda *_: (0,) * nd, pipeline_mode=pl.Buffered(1))


def _mlp(x, oa, ob, wo_a, wo_b, g_mlp, w_up, w_dn, g_fin, *, final):
    B, S, D = x.shape
    tm = MLP_TM
    assert S % tm == 0
    tok = lambda w: pl.BlockSpec((None, tm, w), lambda b, s: (b, s, 0))
    return pl.pallas_call(
        functools.partial(_mlp_kernel, final=final),
        grid=(B, S // tm),
        in_specs=[
            tok(D), tok(oa.shape[-1]), tok(ob.shape[-1]),
            _single_buffered(wo_a.shape), _single_buffered(wo_b.shape), _const_spec(g_mlp.shape),
            _single_buffered(w_up.shape), _single_buffered(w_dn.shape), _const_spec(g_fin.shape),
        ],
        out_specs=tok(D),
        out_shape=jax.ShapeDtypeStruct((B, S, D), _F32),
        compiler_params=pltpu.CompilerParams(
            dimension_semantics=("parallel", "parallel"), vmem_limit_bytes=VMEM_LIMIT_BYTES),
        name="mlp",
    )(x, oa, ob, wo_a, wo_b, g_mlp, w_up, w_dn, g_fin)


def _paired_layout(width):
    q = width // 4
    lane = np.arange(LANES)
    side, r = lane // (LANES // 2), lane % (LANES // 2)
    axis, j = r // q, r % q
    src = np.where(axis < 2, axis * 2 * q + side * q + j, -1)
    return src, j, axis == 0, side == 0


def _take_cols(w, src):
    w_ext = jnp.concatenate([w, jnp.zeros(w.shape[:-1] + (1,), w.dtype)], axis=-1)
    return jnp.take(w_ext, np.where(src < 0, w.shape[-1], src), axis=-1)


def _rotary_tables(S):
    n_rows = S // GRID_W
    row = jnp.arange(n_rows, dtype=_F32)[:, None]
    col = jnp.arange(GRID_W, dtype=_F32)[:, None]
    out = []
    for width in (GQA_HEAD_DIM, MLA_ROPE):
        _, j, by_row, x1_side = _paired_layout(width)
        freqs = ROPE_THETA ** (-(2.0 * j.astype(np.float32)) / (width // 2))
        for fn, sign in ((jnp.cos, 1.0), (jnp.sin, np.where(x1_side, -1.0, 1.0).astype(np.float32))):
            by_r = jnp.broadcast_to((fn(row * freqs) * sign)[:, None, :], (n_rows, GRID_W, LANES))
            by_c = jnp.broadcast_to((fn(col * freqs) * sign)[None, :, :], (n_rows, GRID_W, LANES))
            out.append(jnp.where(by_row, by_r, by_c).reshape(S, LANES))
    return jnp.concatenate(out, axis=1)


def _relayout_weights(w_in, w_mla_q_up):
    o = [0]
    for w in (MLA_Q_LORA, MLA_KV_LORA, MLA_ROPE, GQA_HEADS * GQA_HEAD_DIM,
              GQA_KV_HEADS * GQA_HEAD_DIM, GQA_KV_HEADS * GQA_HEAD_DIM):
        o.append(o[-1] + w)
    cq, ckv, kr, qb, kb, vb = (w_in[..., o[i]:o[i + 1]] for i in range(6))
    src_g = _paired_layout(GQA_HEAD_DIM)[0]
    src_m = _paired_layout(MLA_ROPE)[0]
    heads = lambda w, n: [_take_cols(w[..., h * GQA_HEAD_DIM:(h + 1) * GQA_HEAD_DIM], src_g) for h in range(n)]
    w_in_p = jnp.concatenate([cq, ckv] + heads(qb, GQA_HEADS) + heads(kb, GQA_KV_HEADS)
                             + [vb, _take_cols(kr, src_m)], axis=-1).astype(_BF16)
    L, R, _ = w_mla_q_up.shape
    wq = w_mla_q_up.reshape(L, R, MLA_HEADS, MLA_NOPE + MLA_ROPE)
    wq = jnp.concatenate([wq[..., :MLA_NOPE], _take_cols(wq[..., MLA_NOPE:], src_m)], axis=-1)
    return w_in_p, wq.reshape(L, R, MLA_HEADS * MLA_DK).astype(_BF16)


def _trunk(x, p, tab):
    for l in range(DEPTH):
        qaT, ka, vaT, qbT, kb, vbT = _project(
            x, p["attn_norm"][l], p["w_in"][l], p["mla_q_norm"][l], p["w_qup"][l],
            p["mla_kv_norm"][l], p["w_kvup"][l], p["gqa_q_norm"][l], p["gqa_k_norm"][l], tab)
        oa = _attention(qaT, ka, vaT, tq=MLA_TQ)
        ob = _attention(qbT, kb, vbT, tq=GQA_TQ)
        x = _mlp(x, oa, ob, p["wo_a"][l], p["wo_b"][l], p["mlp_norm"][l], p["w_up"][l], p["w_dn"][l],
                 p["final_norm"], final=(l == DEPTH - 1))
    return x


def kernel(x_prompt, x_sample, attn_norm, w_in, mla_q_norm, w_mla_q_up, mla_kv_norm, w_mla_kv_up,
           gqa_q_norm, gqa_k_norm, w_out, mlp_norm, w_mlp_up, w_mlp_down, final_norm):
    w_in_p, w_qup = _relayout_weights(w_in, w_mla_q_up)
    row = lambda g: g[:, None, :]
    split = MLA_HEADS * MLA_V
    p = {
        "attn_norm": row(attn_norm), "w_in": w_in_p, "mla_q_norm": row(mla_q_norm), "w_qup": w_qup,
        "mla_kv_norm": row(mla_kv_norm), "w_kvup": w_mla_kv_up.astype(_BF16),
        "gqa_q_norm": row(_take_cols(gqa_q_norm, _paired_layout(GQA_HEAD_DIM)[0])),
        "gqa_k_norm": row(_take_cols(gqa_k_norm, _paired_layout(GQA_HEAD_DIM)[0])),
        "wo_a": w_out[:, :split, :].astype(_BF16), "wo_b": w_out[:, split:, :].astype(_BF16),
        "mlp_norm": row(mlp_norm), "w_up": w_mlp_up.astype(_BF16), "w_dn": w_mlp_down.astype(_BF16),
        "final_norm": final_norm[None, :],
    }
    tab = _rotary_tables(max(x_prompt.shape[1], x_sample.shape[1]))
    return _trunk(x_prompt, p, tab), _trunk(x_sample, p, tab)
```
